```python
import jax, jax.numpy as jnp
from jax import lax
import numpy as np

D_MODEL = 1024
BATCH = 4
SEQ = 4096
DEPTH = 4
DEC_BATCH = 128
DEC_SEQ = 8
PAST_LEN = 2048
PAGE_SIZE = 128

DILATED_GROUPS = ((128, 1), (512, 4), (2048, 16))
N_GROUPS = 3
HEADS_PER_GROUP = 4
HEAD_DIM_A = 64
N_HEADS_A = N_GROUPS * HEADS_PER_GROUP
WIDTH_A = N_HEADS_A * HEAD_DIM_A
STEPS_BACK = 128
A_BLOCK = 128

N_HEADS_B = 6
HEAD_DIM_B = 128
WIDTH_B = N_HEADS_B * HEAD_DIM_B
CONV_WIDTH = 4
CONV_DIM = 3 * WIDTH_B
GDN_CHUNK = 64

COL_QA = 0
COL_KA = COL_QA + WIDTH_A
COL_VA = COL_KA + WIDTH_A
COL_ZA = COL_VA + WIDTH_A
COL_QKVB = COL_ZA + WIDTH_A
COL_ZB = COL_QKVB + CONV_DIM
COL_AB = COL_ZB + WIDTH_B
COL_BB = COL_AB + N_HEADS_B
COL_GA = COL_BB + N_HEADS_B
COL_GB = COL_GA + D_MODEL
IN_WIDTH = COL_GB + D_MODEL

DEEPNORM_ALPHA = (2 * DEPTH) ** 0.25
DEEPNORM_BETA = (8 * DEPTH) ** -0.25
LN_EPS = 1e-5
RMS_EPS = 1e-6

kernel_name = 'dilated_swa_gated_deltanet_hybrid_step'


def alibi_slopes():
    i = jnp.arange(1, N_HEADS_A + 1, dtype=jnp.float32)
    return jnp.exp2(-8.0 * i / N_HEADS_A).reshape(N_GROUPS, HEADS_PER_GROUP)


def layer_norm(x, g, b):
    xf = x.astype(jnp.float32)
    mu = xf.mean(-1, keepdims=True)
    var = jnp.mean(jnp.square(xf - mu), -1, keepdims=True)
    return ((xf - mu) * lax.rsqrt(var + LN_EPS) * g.astype(jnp.float32) + b.astype(jnp.float32)).astype(x.dtype)


def l2norm(x):
    return x * lax.rsqrt(jnp.sum(jnp.square(x), -1, keepdims=True) + RMS_EPS)


def softmax_with_lse(s):
    m = s.max(-1, keepdims=True)
    p = jnp.exp(s - m)
    den = p.sum(-1, keepdims=True)
    return p / den, (m + jnp.log(den))[..., 0]


def dilated_group_prompt(q, k, v, dilation, slopes):
    B, S, H, dh = q.shape
    L = S // dilation
    nblk = -(-L // A_BLOCK)
    Lp = nblk * A_BLOCK

    def to_sub(t):
        t = t.reshape(B, L, dilation, H, dh).transpose(0, 2, 1, 3, 4)
        t = jnp.pad(t, ((0, 0), (0, 0), (0, Lp - L), (0, 0), (0, 0)))
        return t.reshape(B, dilation, nblk, A_BLOCK, H, dh)

    def with_prev(t):
        prev = jnp.pad(t[:, :, :-1], ((0, 0), (0, 0), (1, 0), (0, 0), (0, 0), (0, 0)))
        return jnp.concatenate([prev, t], axis=3)

    qs = to_sub(q)
    kk = with_prev(to_sub(k))
    vv = with_prev(to_sub(v))
    s = jnp.einsum('brnqhd,brnkhd->brnhqk', qs, kk, preferred_element_type=jnp.float32) * (dh ** -0.5)
    qi = jnp.arange(A_BLOCK)[:, None]
    kj = jnp.arange(2 * A_BLOCK)[None, :]
    steps = A_BLOCK + qi - kj
    blk = jnp.arange(nblk)[:, None, None]
    valid = (steps >= 0) & (steps <= STEPS_BACK) & (blk * A_BLOCK + kj - A_BLOCK >= 0)
    s = s - slopes[:, None, None] * (steps * dilation).astype(jnp.float32)
    s = jnp.where(valid[:, None], s, -jnp.inf)
    p, lse = softmax_with_lse(s)
    o = jnp.einsum('brnhqk,brnkhd->brnqhd', p, vv.astype(jnp.float32))
    o = o.reshape(B, dilation, Lp, H, dh)[:, :, :L].transpose(0, 2, 1, 3, 4).reshape(B, S, H, dh)
    lse = lse.transpose(0, 1, 2, 4, 3).reshape(B, dilation, Lp, H)[:, :, :L]
    lse = lse.transpose(0, 2, 1, 3).reshape(B, S, H)
    return o.astype(q.dtype), lse


def dilated_group_sample(q, k, v, kv_buf, dilation, slopes):
    B, T, H, dh = q.shape
    Lb = kv_buf.shape[1]
    steps = jnp.arange(STEPS_BACK + 1)
    idx = Lb + jnp.arange(T)[:, None] - steps[None, :] * dilation
    valid = idx >= 0
    new_kv = jnp.stack([k, v], axis=2)
    from_buf = jnp.take(kv_buf, jnp.clip(idx, 0, Lb - 1), axis=1).astype(k.dtype)
    from_new = jnp.take(new_kv, jnp.clip(idx - Lb, 0, T - 1), axis=1)
    kvg = jnp.where((idx < Lb)[None, :, :, None, None, None], from_buf, from_new)
    s = jnp.einsum('bthd,btjhd->bhtj', q, kvg[:, :, :, 0], preferred_element_type=jnp.float32) * (dh ** -0.5)
    s = s - slopes[:, None, None] * (steps * dilation).astype(jnp.float32)
    s = jnp.where(valid, s, -jnp.inf)
    p, lse = softmax_with_lse(s)
    o = jnp.einsum('bhtj,btjhd->bthd', p, kvg[:, :, :, 1].astype(jnp.float32))
    return o.astype(q.dtype), lse.transpose(0, 2, 1), new_kv


def combine_groups(outs, lses):
    alpha = jax.nn.softmax(jnp.stack(lses, 0), axis=0)
    o = jnp.stack(outs, 0) * alpha[..., None].astype(outs[0].dtype)
    G, B, T, H, dh = o.shape
    return o.transpose(1, 2, 0, 3, 4).reshape(B, T, G * H * dh)


def short_conv(u, buf, w):
    T = u.shape[1]
    ext = jnp.concatenate([buf.astype(u.dtype), u], axis=1)
    out = ext[:, 0:T] * w[0]
    for i in range(1, CONV_WIDTH):
        out = out + ext[:, i:i + T] * w[i]
    return jax.nn.silu(out), ext[:, T:]


def gated_delta_rule(q, k, v, g, beta, S0):
    B, T, H, dk = q.shape
    dv = v.shape[-1]
    C = GDN_CHUNK
    n = -(-T // C)
    pad = n * C - T

    def chunks(t):
        t = jnp.pad(t, ((0, 0), (0, pad)) + ((0, 0),) * (t.ndim - 2))
        t = t.reshape((B, n, C) + t.shape[2:])
        return jnp.moveaxis(jnp.moveaxis(t, 3, 2), 1, 0)

    qc, kc, vc, gc, bc = chunks(q), chunks(k), chunks(v), chunks(g), chunks(beta)
    G = jnp.cumsum(gc, axis=-1)
    causal = jnp.tril(jnp.ones((C, C), dtype=bool))
    strict = jnp.tril(jnp.ones((C, C), dtype=bool), -1)
    dmask = jnp.exp(jnp.where(causal, G[..., :, None] - G[..., None, :], -jnp.inf))
    kb = kc * bc[..., None]
    lower = jnp.where(strict, jnp.einsum('...ik,...jk->...ij', kb, kc) * dmask, 0.0)
    rhs = jnp.concatenate([vc * bc[..., None], kb * jnp.exp(G)[..., None]], axis=-1)
    sol = lax.linalg.triangular_solve(jnp.eye(C, dtype=jnp.float32) + lower, rhs, left_side=True, lower=True)
    u, w = sol[..., :dv], sol[..., dv:]
    intra = jnp.einsum('...ik,...jk->...ij', qc, kc) * dmask

    def step(S, xs):
        q_i, k_i, u_i, w_i, G_i, a_i = xs
        v_new = u_i - jnp.einsum('bhck,bhkv->bhcv', w_i, S)
        o = jnp.einsum('bhck,bhkv->bhcv', q_i * jnp.exp(G_i)[..., None], S) + jnp.einsum('bhij,bhjv->bhiv', a_i, v_new)
        g_last = G_i[..., -1]
        S = S * jnp.exp(g_last)[..., None, None] + jnp.einsum(
            'bhck,bhcv->bhkv', k_i * jnp.exp(g_last[..., None] - G_i)[..., None], v_new)
        return S, o

    S, o = lax.scan(step, S0, (qc, kc, u, w, G, intra))
    o = jnp.moveaxis(jnp.moveaxis(o, 0, 1), 2, 3).reshape(B, n * C, H, dv)[:, :T]
    return o, S


def trunk_layer(x, c, w_ada, b_ada, w_in, conv_w, a_log, dt_bias, gdn_norm, w_out_a, w_out_b, w_o, ln_g, ln_b,
                conv_buf, gdn_state, win_bufs):
    B, T, _ = x.shape
    f32 = jnp.float32
    shift, scale, gate = jnp.split((jax.nn.silu(c) @ w_ada + b_ada)[:, None, :], 3, axis=-1)
    h = x * (1 + scale) + shift
    proj = h @ w_in

    head_shape = (B, T, N_GROUPS, HEADS_PER_GROUP, HEAD_DIM_A)
    qa = proj[..., COL_QA:COL_KA].reshape(head_shape)
    ka = proj[..., COL_KA:COL_VA].reshape(head_shape)
    va = proj[..., COL_VA:COL_ZA].reshape(head_shape)
    slopes = alibi_slopes()
    outs, lses, new_kv = [], [], []
    for gi, (window, dil) in enumerate(DILATED_GROUPS):
        if win_bufs is None:
            o, lse = dilated_group_prompt(qa[:, :, gi], ka[:, :, gi], va[:, :, gi], dil, slopes[gi])
            keep = min(window, T)
            new_kv.append(jnp.stack([ka[:, T - keep:, gi], va[:, T - keep:, gi]], axis=2))
        else:
            o, lse, kv = dilated_group_sample(qa[:, :, gi], ka[:, :, gi], va[:, :, gi], win_bufs[gi], dil, slopes[gi])
            new_kv.append(kv)
        outs.append(o)
        lses.append(lse)
    att_a = combine_groups(outs, lses)
    branch_a = (att_a * jax.nn.silu(proj[..., COL_ZA:COL_QKVB])) @ w_out_a

    qkv, new_conv = short_conv(proj[..., COL_QKVB:COL_ZB], conv_buf, conv_w)
    qkv = qkv.astype(f32).reshape(B, T, 3, N_HEADS_B, HEAD_DIM_B)
    qb = l2norm(qkv[:, :, 0]) * (HEAD_DIM_B ** -0.5)
    kb = l2norm(qkv[:, :, 1])
    vb = qkv[:, :, 2]
    g = -jnp.exp(a_log.astype(f32)) * jax.nn.softplus(proj[..., COL_AB:COL_BB].astype(f32) + dt_bias.astype(f32))
    beta = jax.nn.sigmoid(proj[..., COL_BB:COL_GA].astype(f32))
    o_b, S_new = gated_delta_rule(qb, kb, vb, g, beta, gdn_state.astype(f32))
    o_b = o_b * lax.rsqrt(jnp.mean(jnp.square(o_b), -1, keepdims=True) + RMS_EPS) * gdn_norm.astype(f32)
    o_b = o_b.reshape(B, T, WIDTH_B).astype(x.dtype)
    branch_b = (o_b * jax.nn.silu(proj[..., COL_ZB:COL_AB])) @ w_out_b

    merged = jax.nn.sigmoid(proj[..., COL_GA:COL_GB]) * branch_a + jax.nn.sigmoid(proj[..., COL_GB:]) * branch_b
    out = merged @ w_o
    x_new = layer_norm(DEEPNORM_ALPHA * x + gate * out, ln_g, ln_b)
    return x_new, new_kv, new_conv, S_new


def setup_inputs(seed: int = 0) -> dict:
    key = jax.random.key(seed)
    ks = jax.random.split(key, 21)
    f32 = jnp.float32
    nrm = lambda k, shape, s: jax.random.normal(k, shape, f32) * s
    win_len = [min(w, PAST_LEN) for (w, _) in DILATED_GROUPS]
    win_shape = lambda L: (DEPTH, DEC_BATCH, L, 2, HEADS_PER_GROUP, HEAD_DIM_A)
    dt = jnp.exp(jax.random.uniform(ks[14], (DEPTH, N_HEADS_B), f32, np.log(1e-3), np.log(1e-1)))
    return {
        'x_prompt': nrm(ks[0], (BATCH, SEQ, D_MODEL), 1.0),
        'x_sample': nrm(ks[1], (DEC_BATCH, DEC_SEQ, D_MODEL), 1.0),
        'c_prompt': nrm(ks[2], (BATCH, D_MODEL), 1.0),
        'c_sample': nrm(ks[3], (DEC_BATCH, D_MODEL), 1.0),
        'cache_win1': nrm(ks[4], win_shape(win_len[0]), 1.0),
        'cache_win2': nrm(ks[5], win_shape(win_len[1]), 1.0),
        'cache_win3': nrm(ks[6], win_shape(win_len[2]), 1.0),
        'state_gdn': nrm(ks[7], (DEPTH, DEC_BATCH, N_HEADS_B, HEAD_DIM_B, HEAD_DIM_B), 0.2),
        'state_conv': nrm(ks[8], (DEPTH, DEC_BATCH, CONV_WIDTH - 1, CONV_DIM), 1.0),
        'w_ada': nrm(ks[9], (DEPTH, D_MODEL, 3 * D_MODEL), 0.5 * D_MODEL ** -0.5),
        'b_ada': nrm(ks[10], (DEPTH, 3 * D_MODEL), 0.01),
        'w_in': nrm(ks[11], (DEPTH, D_MODEL, IN_WIDTH), D_MODEL ** -0.5),
        'conv_w': nrm(ks[12], (DEPTH, CONV_WIDTH, CONV_DIM), CONV_WIDTH ** -0.5),
        'a_log': jnp.log(jax.random.uniform(ks[13], (DEPTH, N_HEADS_B), f32, 1.0, 16.0)),
        'dt_bias': dt + jnp.log(-jnp.expm1(-dt)),
        'gdn_norm': 1.0 + nrm(ks[15], (DEPTH, HEAD_DIM_B), 0.02),
        'w_out_a': nrm(ks[16], (DEPTH, WIDTH_A, D_MODEL), WIDTH_A ** -0.5),
        'w_out_b': nrm(ks[17], (DEPTH, WIDTH_B, D_MODEL), WIDTH_B ** -0.5),
        'w_o': nrm(ks[18], (DEPTH, D_MODEL, D_MODEL), DEEPNORM_BETA * D_MODEL ** -0.5),
        'ln_g': 1.0 + nrm(ks[19], (DEPTH, D_MODEL), 0.02),
        'ln_b': nrm(ks[20], (DEPTH, D_MODEL), 0.02),
    }


def reference(x_prompt, x_sample, c_prompt, c_sample, cache_win1, cache_win2, cache_win3, state_gdn, state_conv,
              w_ada, b_ada, w_in, conv_w, a_log, dt_bias, gdn_norm, w_out_a, w_out_b, w_o, ln_g, ln_b):
    win_caches = (cache_win1, cache_win2, cache_win3)
    xp, xs = x_prompt, x_sample
    n_p = xp.shape[0]
    zero_conv = jnp.zeros((n_p, CONV_WIDTH - 1, CONV_DIM), x_prompt.dtype)
    zero_state = jnp.zeros((n_p, N_HEADS_B, HEAD_DIM_B, HEAD_DIM_B), jnp.float32)
    acc_p = [[] for _ in range(N_GROUPS + 2)]
    acc_s = [[] for _ in range(N_GROUPS + 2)]
    for l in range(DEPTH):
        lw = (w_ada[l], b_ada[l], w_in[l], conv_w[l], a_log[l], dt_bias[l], gdn_norm[l],
              w_out_a[l], w_out_b[l], w_o[l], ln_g[l], ln_b[l])
        xp, kv_p, cv_p, s_p = trunk_layer(xp, c_prompt, *lw, zero_conv, zero_state, None)
        xs, kv_s, cv_s, s_s = trunk_layer(xs, c_sample, *lw, state_conv[l], state_gdn[l],
                                          (cache_win1[l], cache_win2[l], cache_win3[l]))
        for i, a in enumerate(list(kv_p) + [s_p.astype(x_prompt.dtype), cv_p]):
            acc_p[i].append(a)
        for i, a in enumerate(list(kv_s) + [s_s.astype(state_gdn.dtype), cv_s]):
            acc_s[i].append(a)
    win1_prompt, win2_prompt, win3_prompt, gdn_prompt, conv_prompt = [jnp.stack(a) for a in acc_p]
    win1_sample, win2_sample, win3_sample, gdn_sample, conv_sample = [jnp.stack(a) for a in acc_s]
    return (xp, xs, win1_prompt, win2_prompt, win3_prompt, gdn_prompt, conv_prompt,
            win1_sample, win2_sample, win3_sample, gdn_sample, conv_sample)
```

```python
import functools

import numpy as np
import jax
import jax.numpy as jnp
from jax import lax
from jax.experimental import pallas as pl
from jax.experimental.pallas import tpu as pltpu

F32 = jnp.float32
BF16 = jnp.bfloat16
HIGHEST = lax.Precision.HIGHEST

D_MODEL = 1024
DEPTH = 4
DILATIONS = (1, 4, 16)
WINDOWS = (128, 512, 2048)
N_GROUPS = 3
HEADS_PER_GROUP = 4
HEAD_DIM_A = 64
GROUP_W = HEADS_PER_GROUP * HEAD_DIM_A
WIDTH_A = N_GROUPS * GROUP_W
N_HEADS_A = N_GROUPS * HEADS_PER_GROUP
STEPS_BACK = 128
A_BLOCK = 128
N_HEADS_B = 6
HEAD_DIM_B = 128
WIDTH_B = N_HEADS_B * HEAD_DIM_B
CONV_WIDTH = 4
CONV_DIM = 3 * WIDTH_B
GDN_CHUNK = 64
IN_WIDTH = 8204
DEEPNORM_ALPHA = (2 * DEPTH) ** 0.25
LN_EPS = 1e-5
RMS_EPS = 1e-6

COL_QA = 0
COL_KA = 768
COL_VA = 1536
COL_ZA = 2304
COL_QKVB = 3072
COL_ZB = 5376
COL_GA = 6144
COL_GB = 7168
COL_AB = 8192
WP = 8448
N_COL_TILES = 3
COL_TILE = WP // N_COL_TILES

VMEM_LIMIT = 48 * 1024 * 1024


def _cparams(sem):
    return pltpu.CompilerParams(dimension_semantics=sem, vmem_limit_bytes=VMEM_LIMIT)


def _silu(x):
    return x * jax.nn.sigmoid(x)


def _mod_kernel(c_ref, w_ref, b_ref, o_ref):
    a = _silu(c_ref[...]).astype(BF16)
    o_ref[0] = jnp.dot(a, w_ref[0].astype(BF16), preferred_element_type=F32) + b_ref[0]


def _modulation(c_all, w_ada, b_ada):
    n = c_all.shape[0]
    return pl.pallas_call(
        _mod_kernel,
        out_shape=jax.ShapeDtypeStruct((DEPTH, n, 3 * D_MODEL), F32),
        grid=(DEPTH, 3),
        in_specs=[
            pl.BlockSpec((n, D_MODEL), lambda l, j: (0, 0)),
            pl.BlockSpec((1, D_MODEL, D_MODEL), lambda l, j: (l, 0, j)),
            pl.BlockSpec((1, 1, D_MODEL), lambda l, j: (l, 0, j)),
        ],
        out_specs=pl.BlockSpec((1, n, D_MODEL), lambda l, j: (l, 0, j)),
        compiler_params=_cparams(("arbitrary", "arbitrary")),
        name="adaln_modulation",
    )(c_all, w_ada, b_ada.reshape(DEPTH, 1, 3 * D_MODEL))


def _inproj_kernel(x_ref, mod_ref, w_ref, o_ref):
    bs, bt, _ = x_ref.shape
    shift = mod_ref[:, :, 0:D_MODEL]
    scale = mod_ref[:, :, D_MODEL:2 * D_MODEL]
    h = x_ref[...] * (1.0 + scale) + shift
    h = h.reshape(bs * bt, D_MODEL).astype(BF16)
    o_ref[...] = jnp.dot(h, w_ref[...], preferred_element_type=F32).reshape(bs, bt, COL_TILE)


def _in_projection(x, mod, w, bs, bt):
    nseq, t, _ = x.shape
    return pl.pallas_call(
        _inproj_kernel,
        out_shape=jax.ShapeDtypeStruct((nseq, t, WP), F32),
        grid=(N_COL_TILES, nseq // bs, t // bt),
        in_specs=[
            pl.BlockSpec((bs, bt, D_MODEL), lambda j, i, k: (i, k, 0)),
            pl.BlockSpec((bs, 1, 3 * D_MODEL), lambda j, i, k: (i, 0, 0)),
            pl.BlockSpec((D_MODEL, COL_TILE), lambda j, i, k: (0, j)),
        ],
        out_specs=pl.BlockSpec((bs, bt, COL_TILE), lambda j, i, k: (i, k, j)),
        compiler_params=_cparams(("arbitrary", "arbitrary", "arbitrary")),
        name="in_projection",
    )(x, mod, w)


def _attn_prompt_kernel(slopes_ref, q_ref, kp_ref, ko_ref, vp_ref, vo_ref, o_ref, lse_ref, *, group, dilation):
    n = pl.program_id(2)
    q = (q_ref[0] * (HEAD_DIM_A ** -0.5)).astype(BF16)
    k = jnp.concatenate([kp_ref[0], ko_ref[0]], axis=0).astype(BF16)
    v = jnp.concatenate([vp_ref[0], vo_ref[0]], axis=0).astype(BF16)
    qi = lax.broadcasted_iota(jnp.int32, (A_BLOCK, 2 * A_BLOCK), 0)
    kj = lax.broadcasted_iota(jnp.int32, (A_BLOCK, 2 * A_BLOCK), 1)
    steps = A_BLOCK + qi - kj
    valid = (steps >= 0) & (steps <= STEPS_BACK) & ((kj >= A_BLOCK) | (n > 0))
    dist = (steps * dilation).astype(F32)
    lane_head = lax.broadcasted_iota(jnp.int32, (1, GROUP_W), 1) // HEAD_DIM_A
    o_acc = jnp.zeros((A_BLOCK, GROUP_W), F32)
    lse_acc = jnp.zeros((A_BLOCK, GROUP_W), F32)
    for h in range(HEADS_PER_GROUP):
        head = lane_head == h
        qh = jnp.where(head, q, jnp.zeros_like(q))
        s = lax.dot_general(qh, k, (((1,), (1,)), ((), ())), preferred_element_type=F32)
        s = jnp.where(valid, s - slopes_ref[group, h] * dist, -jnp.inf)
        m = s.max(-1, keepdims=True)
        p = jnp.exp(s - m)
        den = p.sum(-1, keepdims=True)
        oh = jnp.dot((p / den).astype(BF16), v, preferred_element_type=F32)
        o_acc = jnp.where(head, oh, o_acc)
        lse_acc = jnp.where(head, m + jnp.log(den), lse_acc)
    o_ref[0] = o_acc
    lse_ref[0] = lse_acc


def _attention_prompt(proj, slopes, group):
    b, s, _ = proj.shape
    d = DILATIONS[group]
    length = s // d
    nblk = length // A_BLOCK
    cpr = WP // GROUP_W
    view = proj.reshape(b, length, d * WP)
    qb, kb, vb = group, COL_KA // GROUP_W + group, COL_VA // GROUP_W + group

    def blk(col, prev):
        if prev:
            return pl.BlockSpec((1, A_BLOCK, GROUP_W), lambda i, r, n: (i, jnp.maximum(n - 1, 0), r * cpr + col))
        return pl.BlockSpec((1, A_BLOCK, GROUP_W), lambda i, r, n: (i, n, r * cpr + col))

    out_spec = pl.BlockSpec((1, A_BLOCK, GROUP_W), lambda i, r, n: (i, n, r))
    o, lse = pl.pallas_call(
        functools.partial(_attn_prompt_kernel, group=group, dilation=d),
        out_shape=[jax.ShapeDtypeStruct((b, length, d * GROUP_W), F32)] * 2,
        grid=(b, d, nblk),
        in_specs=[pl.BlockSpec(memory_space=pltpu.SMEM),
                  blk(qb, False), blk(kb, True), blk(kb, False), blk(vb, True), blk(vb, False)],
        out_specs=[out_spec, out_spec],
        compiler_params=_cparams(("arbitrary", "arbitrary", "arbitrary")),
        name=f"attn_prompt_g{group}",
    )(slopes, view, view, view, view, view)
    return o.reshape(b, s, GROUP_W), lse.reshape(b, s, GROUP_W)


LANES = 128


def _sample_bias(slopes, group, t_new):
    d = DILATIONS[group]
    lb = WINDOWS[group]
    if d >= t_new:
        i = np.arange(lb // d)[:, None]
        r = np.arange(t_new)[None, :]
        pos = (d * i + r).reshape(-1)
    else:
        pos = np.arange(lb)
    t = np.arange(t_new)

    def table(key_pos, n_rows):
        diff = (lb + t)[None, :] - key_pos[:, None]
        ok = (diff >= 0) & (diff % d == 0) & (diff // d <= STEPS_BACK)
        dist = jnp.asarray(np.where(ok, diff, 0), F32)
        bias = jnp.where(jnp.asarray(ok)[:, None, :], -slopes[group][None, :, None] * dist[:, None, :], -jnp.inf)
        bias = bias.reshape(key_pos.shape[0], HEADS_PER_GROUP * t_new)
        full = jnp.zeros((n_rows, LANES), F32)
        full = full.at[:, :HEADS_PER_GROUP * t_new].set(-jnp.inf)
        return full.at[:key_pos.shape[0], :HEADS_PER_GROUP * t_new].set(bias)

    return table(pos, pos.shape[0]), table(lb + t, LANES)


def _attn_sample_kernel(q_ref, k_ref, v_ref, cache_ref, bias_ref, biasn_ref, o_ref, lse_ref, *, bb, t_new):
    nq = HEADS_PER_GROUP * t_new
    row_head = lax.broadcasted_iota(jnp.int32, (nq, GROUP_W), 0) // t_new
    lane_head2 = lax.broadcasted_iota(jnp.int32, (nq, GROUP_W), 1) // HEAD_DIM_A
    lane_head = lax.broadcasted_iota(jnp.int32, (1, GROUP_W), 1) // HEAD_DIM_A
    eye = (lax.broadcasted_iota(jnp.int32, (LANES, LANES), 0) == lax.broadcasted_iota(jnp.int32, (LANES, LANES), 1))
    pad_q = jnp.zeros((LANES - nq, GROUP_W), F32)
    pad_n = jnp.zeros((LANES - t_new, GROUP_W), F32)
    nt = (((1,), (1,)), ((), ()))

    def column(row):
        return jnp.sum(jnp.where(eye, row, 0.0), axis=1, keepdims=True)

    def body(i, carry):
        q = q_ref[i] * (HEAD_DIM_A ** -0.5)
        qm = jnp.where(row_head == lane_head2, jnp.concatenate([q] * HEADS_PER_GROUP, axis=0), 0.0)
        qm = jnp.concatenate([qm, pad_q], axis=0).astype(BF16)
        rows = cache_ref[i]
        rows = rows.reshape(-1, 2 * GROUP_W)
        kc = rows[:, :GROUP_W].astype(BF16)
        vc = rows[:, GROUP_W:].astype(BF16)
        kn = jnp.concatenate([k_ref[i], pad_n], axis=0).astype(BF16)
        vn = jnp.concatenate([v_ref[i], pad_n], axis=0).astype(BF16)
        sc = lax.dot_general(kc, qm, nt, preferred_element_type=F32) + bias_ref[...]
        sn = lax.dot_general(kn, qm, nt, preferred_element_type=F32) + biasn_ref[...]
        m = jnp.maximum(sc.max(0, keepdims=True), sn.max(0, keepdims=True))
        pc = jnp.exp(sc - m)
        pn = jnp.exp(sn - m)
        den = pc.sum(0, keepdims=True) + pn.sum(0, keepdims=True)
        inv = 1.0 / den
        pc = (pc * inv).T.astype(BF16)
        pn = (pn * inv).T.astype(BF16)
        of = jnp.dot(pc, vc, preferred_element_type=F32) + jnp.dot(pn, vn, preferred_element_type=F32)
        lse_col = column(m + jnp.log(den))
        o = jnp.zeros((t_new, GROUP_W), F32)
        lse = jnp.zeros((t_new, GROUP_W), F32)
        for h in range(HEADS_PER_GROUP):
            head = lane_head == h
            o = jnp.where(head, of[h * t_new:(h + 1) * t_new, :], o)
            lse = jnp.where(head, lse_col[h * t_new:(h + 1) * t_new, :], lse)
        o_ref[i] = o
        lse_ref[i] = lse
        return carry

    lax.fori_loop(0, bb, body, 0)


def _attention_sample(proj, cache, slopes, group, bb):
    nseq, t_new, _ = proj.shape
    d = DILATIONS[group]
    lb = WINDOWS[group]
    row_w = 2 * GROUP_W
    if d >= t_new:
        view = cache.reshape(nseq, lb // d, d, row_w)
        cache_spec = pl.BlockSpec((bb, lb // d, t_new, row_w), lambda i: (i, 0, 0, 0))
    else:
        view = cache.reshape(nseq, lb, row_w)
        cache_spec = pl.BlockSpec((bb, lb, row_w), lambda i: (i, 0, 0))
    bias, bias_new = _sample_bias(slopes, group, t_new)
    qb, kb, vb = group, COL_KA // GROUP_W + group, COL_VA // GROUP_W + group
    new_spec = lambda col: pl.BlockSpec((bb, t_new, GROUP_W), lambda i: (i, 0, col))
    out_spec = pl.BlockSpec((bb, t_new, GROUP_W), lambda i: (i, 0, 0))
    return pl.pallas_call(
        functools.partial(_attn_sample_kernel, bb=bb, t_new=t_new),
        out_shape=[jax.ShapeDtypeStruct((nseq, t_new, GROUP_W), F32)] * 2,
        grid=(nseq // bb,),
        in_specs=[new_spec(qb), new_spec(kb), new_spec(vb), cache_spec,
                  pl.BlockSpec(bias.shape, lambda i: (0, 0)), pl.BlockSpec(bias_new.shape, lambda i: (0, 0))],
        out_specs=[out_spec, out_spec],
        compiler_params=_cparams(("arbitrary",)),
        name=f"attn_sample_g{group}",
    )(proj, proj, proj, view, bias, bias_new)


def _gdn_kernel(q_ref, k_ref, v_ref, ab_ref, conv0_ref, s0_ref, cw_ref, par_ref, o_ref, sout_ref, ext_ref, s_ref,
                *, bb, chunk):
    c_idx = pl.program_id(1)
    C = chunk

    @pl.when(c_idx == 0)
    def _():
        ext_ref[:, 0:8, :] = conv0_ref[...]
        s_ref[...] = s0_ref[...]

    @pl.when(c_idx > 0)
    def _():
        ext_ref[:, 0:8, :] = ext_ref[:, C:C + 8, :]

    ext_ref[:, 8:8 + C, 0:WIDTH_B] = q_ref[...]
    ext_ref[:, 8:8 + C, WIDTH_B:2 * WIDTH_B] = k_ref[...]
    ext_ref[:, 8:8 + C, 2 * WIDTH_B:3 * WIDTH_B] = v_ref[...]

    ri = lax.broadcasted_iota(jnp.int32, (C, C), 0)
    ci = lax.broadcasted_iota(jnp.int32, (C, C), 1)
    causal = ri >= ci
    strict = ri > ci
    tril = causal.astype(F32)
    ident = (ri == ci).astype(F32)
    neg_a = -jnp.exp(par_ref[0:1, :])
    dt_bias = par_ref[1:2, :]
    norm_w = par_ref[2:3, :]
    pad_rows = jnp.zeros((HEAD_DIM_B - C, HEAD_DIM_B), F32) if C < HEAD_DIM_B else None
    nt = (((1,), (1,)), ((), ()))
    n_double = int(np.log2(C)) - 1

    def mm(a, b):
        return jnp.dot(a, b, preferred_element_type=F32, precision=HIGHEST)

    def conv(i, col):
        acc = ext_ref[i, pl.ds(8 - (CONV_WIDTH - 1), C), pl.ds(col, HEAD_DIM_B)] * cw_ref[0:1, pl.ds(col, HEAD_DIM_B)]
        for j in range(1, CONV_WIDTH):
            acc = acc + (ext_ref[i, pl.ds(8 - (CONV_WIDTH - 1) + j, C), pl.ds(col, HEAD_DIM_B)]
                         * cw_ref[j:j + 1, pl.ds(col, HEAD_DIM_B)])
        return _silu(acc)

    def l2n(x):
        return x * lax.rsqrt(jnp.sum(x * x, axis=-1, keepdims=True) + RMS_EPS)

    def body(i, carry):
        ab = ab_ref[i]
        z = ab + dt_bias
        softplus = jnp.maximum(z, 0.0) + jnp.log(1.0 + jnp.exp(-jnp.abs(z)))
        g_all = neg_a * softplus
        beta_all = jax.nn.sigmoid(ab)
        gcum_all = mm(tril, g_all)
        for h in range(N_HEADS_B):
            col = h * HEAD_DIM_B
            q = l2n(conv(i, col)) * (HEAD_DIM_B ** -0.5)
            k = l2n(conv(i, WIDTH_B + col))
            v = conv(i, 2 * WIDTH_B + col)
            g = g_all[:, h:h + 1]
            beta = beta_all[:, N_HEADS_B + h:N_HEADS_B + h + 1]
            gcum = gcum_all[:, h:h + 1]
            diff = mm(tril, jnp.where(ri > ci, g, 0.0))
            decay = jnp.where(causal, jnp.exp(diff), 0.0)
            kbeta = k * beta
            lower = jnp.where(strict, lax.dot_general(kbeta, k, nt, preferred_element_type=F32, precision=HIGHEST)
                              * decay, 0.0)
            inv = ident - lower
            power = lower
            for _ in range(n_double):
                power = mm(power, power)
                inv = inv + mm(inv, power)
            e_g = jnp.exp(gcum)
            u = mm(inv, v * beta)
            w = mm(inv, kbeta * e_g)
            intra = lax.dot_general(q, k, nt, preferred_element_type=F32, precision=HIGHEST) * decay
            state = s_ref[i, h]
            v_new = u - mm(w, state)
            o = mm(q * e_g, state) + mm(intra, v_new)
            g_last = gcum[C - 1:C, :]
            k_dec = k * jnp.exp(g_last - gcum)
            if pad_rows is not None:
                k_dec_p = jnp.concatenate([k_dec, pad_rows], axis=0)
                v_new_p = jnp.concatenate([v_new, pad_rows], axis=0)
            else:
                k_dec_p, v_new_p = k_dec, v_new
            s_ref[i, h] = state * jnp.exp(g_last) + mm(k_dec_p.T, v_new_p)
            o = o * lax.rsqrt(jnp.mean(o * o, axis=-1, keepdims=True) + RMS_EPS) * norm_w
            o_ref[i, :, pl.ds(col, HEAD_DIM_B)] = o
        return carry

    lax.fori_loop(0, bb, body, 0)

    @pl.when(c_idx == pl.num_programs(1) - 1)
    def _():
        sout_ref[...] = s_ref[...]


def _gated_deltanet(proj, conv0, s0, conv_w, params, bb, chunk):
    nseq, t, _ = proj.shape
    qb = COL_QKVB // WIDTH_B
    col_spec = lambda col: pl.BlockSpec((bb, chunk, WIDTH_B), lambda i, c: (i, c, col))
    state_spec = pl.BlockSpec((bb, N_HEADS_B, HEAD_DIM_B, HEAD_DIM_B), lambda i, c: (i, 0, 0, 0))
    return pl.pallas_call(
        functools.partial(_gdn_kernel, bb=bb, chunk=chunk),
        out_shape=[jax.ShapeDtypeStruct((nseq, t, WIDTH_B), F32),
                   jax.ShapeDtypeStruct((nseq, N_HEADS_B, HEAD_DIM_B, HEAD_DIM_B), F32)],
        grid=(nseq // bb, t // chunk),
        in_specs=[col_spec(qb), col_spec(qb + 1), col_spec(qb + 2),
                  pl.BlockSpec((bb, chunk, LANES), lambda i, c: (i, c, COL_AB // LANES)),
                  pl.BlockSpec((bb, 8, CONV_DIM), lambda i, c: (i, 0, 0)),
                  state_spec,
                  pl.BlockSpec((8, CONV_DIM), lambda i, c: (0, 0)),
                  pl.BlockSpec((8, LANES), lambda i, c: (0, 0))],
        out_specs=[pl.BlockSpec((bb, chunk, WIDTH_B), lambda i, c: (i, c, 0)), state_spec],
        scratch_shapes=[pltpu.VMEM((bb, chunk + 8, CONV_DIM), F32),
                        pltpu.VMEM((bb, N_HEADS_B, HEAD_DIM_B, HEAD_DIM_B), F32)],
        compiler_params=_cparams(("arbitrary", "arbitrary")),
        name="gated_deltanet",
    )(proj, proj, proj, proj, conv0, s0, conv_w, params)


def _out_kernel(x_ref, mod_ref, o1_ref, o2_ref, o3_ref, l1_ref, l2_ref, l3_ref, za_ref, ob_ref, zb_ref, ga_ref, gb_ref,
                wa_ref, wb_ref, wo_ref, lng_ref, lnb_ref, y_ref):
    bs, bt, _ = x_ref.shape
    rows = bs * bt
    flat = lambda ref: ref[...].reshape(rows, ref.shape[-1])
    l1, l2, l3 = flat(l1_ref), flat(l2_ref), flat(l3_ref)
    m = jnp.maximum(jnp.maximum(l1, l2), l3)
    e1, e2, e3 = jnp.exp(l1 - m), jnp.exp(l2 - m), jnp.exp(l3 - m)
    tot = e1 + e2 + e3
    att = jnp.concatenate([flat(o1_ref) * (e1 / tot), flat(o2_ref) * (e2 / tot), flat(o3_ref) * (e3 / tot)], axis=-1)
    branch_a = jnp.dot((att * _silu(flat(za_ref))).astype(BF16), wa_ref[...], preferred_element_type=F32)
    branch_b = jnp.dot((flat(ob_ref) * _silu(flat(zb_ref))).astype(BF16), wb_ref[...], preferred_element_type=F32)
    merged = jax.nn.sigmoid(flat(ga_ref)) * branch_a + jax.nn.sigmoid(flat(gb_ref)) * branch_b
    out = jnp.dot(merged.astype(BF16), wo_ref[...], preferred_element_type=F32).reshape(bs, bt, D_MODEL)
    gate = mod_ref[:, :, 2 * D_MODEL:3 * D_MODEL]
    y = DEEPNORM_ALPHA * x_ref[...] + gate * out
    mu = jnp.mean(y, axis=-1, keepdims=True)
    yc = y - mu
    var = jnp.mean(yc * yc, axis=-1, keepdims=True)
    y_ref[...] = yc * lax.rsqrt(var + LN_EPS) * lng_ref[...] + lnb_ref[...]


def _output_block(x, mod, o_g, lse_g, o_b, proj, wa, wb, wo, ln_g, ln_b, bs, bt):
    nseq, t, _ = x.shape
    row = lambda w, col: pl.BlockSpec((bs, bt, w), lambda i, k: (i, k, col))
    const = lambda shape: pl.BlockSpec(shape, lambda i, k: (0,) * len(shape))
    return pl.pallas_call(
        _out_kernel,
        out_shape=jax.ShapeDtypeStruct((nseq, t, D_MODEL), F32),
        grid=(nseq // bs, t // bt),
        in_specs=[row(D_MODEL, 0),
                  pl.BlockSpec((bs, 1, 3 * D_MODEL), lambda i, k: (i, 0, 0)),
                  row(GROUP_W, 0), row(GROUP_W, 0), row(GROUP_W, 0),
                  row(GROUP_W, 0), row(GROUP_W, 0), row(GROUP_W, 0),
                  row(WIDTH_A, COL_ZA // WIDTH_A),
                  row(WIDTH_B, 0),
                  row(WIDTH_B, COL_ZB // WIDTH_B),
                  row(D_MODEL, COL_GA // D_MODEL),
                  row(D_MODEL, COL_GB // D_MODEL),
                  const((WIDTH_A, D_MODEL)), const((WIDTH_B, D_MODEL)), const((D_MODEL, D_MODEL)),
                  const((1, 1, D_MODEL)), const((1, 1, D_MODEL))],
        out_specs=row(D_MODEL, 0),
        compiler_params=_cparams(("arbitrary", "arbitrary")),
        name="output_block",
    )(x, mod, o_g[0], o_g[1], o_g[2], lse_g[0], lse_g[1], lse_g[2], proj, o_b, proj, proj, proj,
      wa, wb, wo, ln_g.reshape(1, 1, D_MODEL), ln_b.reshape(1, 1, D_MODEL))


def _alibi_slopes():
    i = jnp.arange(1, N_HEADS_A + 1, dtype=F32)
    return jnp.exp2(-8.0 * i / N_HEADS_A).reshape(N_GROUPS, HEADS_PER_GROUP)


def _reorder_w_in(w_in):
    main = w_in[:, :, :COL_GA]
    ab = w_in[:, :, 6144:6156]
    ga = w_in[:, :, 6156:7180]
    gb = w_in[:, :, 7180:8204]
    pad = jnp.zeros(w_in.shape[:2] + (WP - COL_AB - ab.shape[-1],), w_in.dtype)
    return jnp.concatenate([main, ga, gb, ab, pad], axis=-1).astype(BF16)


def _split_kv(proj, group, start):
    b = proj.shape[0]
    k = proj[:, start:, COL_KA + group * GROUP_W:COL_KA + (group + 1) * GROUP_W]
    v = proj[:, start:, COL_VA + group * GROUP_W:COL_VA + (group + 1) * GROUP_W]
    kv = jnp.stack([k, v], axis=2)
    return kv.reshape(b, kv.shape[1], 2, HEADS_PER_GROUP, HEAD_DIM_A)


def kernel(x_prompt, x_sample, c_prompt, c_sample, cache_win1, cache_win2, cache_win3, state_gdn, state_conv,
           w_ada, b_ada, w_in, conv_w, a_log, dt_bias, gdn_norm, w_out_a, w_out_b, w_o, ln_g, ln_b):
    n_p, seq, _ = x_prompt.shape
    n_s, t_new, _ = x_sample.shape
    caches = (cache_win1, cache_win2, cache_win3)
    slopes = _alibi_slopes()

    w_in_p = _reorder_w_in(w_in)
    wa = w_out_a.astype(BF16)
    wb = w_out_b.astype(BF16)
    wo = w_o.astype(BF16)
    conv_w_p = jnp.pad(conv_w, ((0, 0), (0, 8 - CONV_WIDTH), (0, 0)))
    params = jnp.zeros((DEPTH, 8, LANES), F32)
    params = params.at[:, 0, :N_HEADS_B].set(a_log).at[:, 1, :N_HEADS_B].set(dt_bias).at[:, 2, :].set(gdn_norm)

    n_all = n_p + n_s
    n_pad = -(-n_all // 8) * 8
    c_all = jnp.pad(jnp.concatenate([c_prompt, c_sample], axis=0), ((0, n_pad - n_all), (0, 0)))
    mod_all = _modulation(c_all, w_ada, b_ada)

    zero_conv = jnp.zeros((n_p, 8, CONV_DIM), F32)
    zero_state = jnp.zeros((n_p, N_HEADS_B, HEAD_DIM_B, HEAD_DIM_B), F32)
    conv_s = jnp.pad(state_conv, ((0, 0), (0, 0), (8 - (CONV_WIDTH - 1), 0), (0, 0)))

    xp, xs = x_prompt, x_sample
    acc_p = [[] for _ in range(N_GROUPS + 2)]
    acc_s = [[] for _ in range(N_GROUPS + 2)]
    for l in range(DEPTH):
        mod_p = mod_all[l, :n_p, None, :]
        mod_s = mod_all[l, n_p:n_all, None, :]

        proj_p = _in_projection(xp, mod_p, w_in_p[l], 1, 512)
        proj_s = _in_projection(xs, mod_s, w_in_p[l], 32, t_new)

        o_p, lse_p, o_s, lse_s = [], [], [], []
        for g in range(N_GROUPS):
            o, lse = _attention_prompt(proj_p, slopes, g)
            o_p.append(o)
            lse_p.append(lse)
            o, lse = _attention_sample(proj_s, caches[g][l], slopes, g, 8)
            o_s.append(o)
            lse_s.append(lse)

        ob_p, st_p = _gated_deltanet(proj_p, zero_conv, zero_state, conv_w_p[l], params[l], 1, GDN_CHUNK)
        ob_s, st_s = _gated_deltanet(proj_s, conv_s[l], state_gdn[l], conv_w_p[l], params[l], 8, t_new)

        xp_new = _output_block(xp, mod_p, o_p, lse_p, ob_p, proj_p, wa[l], wb[l], wo[l], ln_g[l], ln_b[l], 1, 256)
        xs_new = _output_block(xs, mod_s, o_s, lse_s, ob_s, proj_s, wa[l], wb[l], wo[l], ln_g[l], ln_b[l], 32, t_new)

        for g in range(N_GROUPS):
            acc_p[g].append(_split_kv(proj_p, g, seq - min(WINDOWS[g], seq)))
            acc_s[g].append(_split_kv(proj_s, g, 0))
        acc_p[N_GROUPS].append(st_p)
        acc_s[N_GROUPS].append(st_s)
        acc_p[N_GROUPS + 1].append(proj_p[:, seq - (CONV_WIDTH - 1):, COL_QKVB:COL_QKVB + CONV_DIM])
        acc_s[N_GROUPS + 1].append(proj_s[:, t_new - (CONV_WIDTH - 1):, COL_QKVB:COL_QKVB + CONV_DIM])
        xp, xs = xp_new, xs_new

    outs_p = [jnp.stack(a) for a in acc_p]
    outs_s = [jnp.stack(a) for a in acc_s]
    return (xp, xs, *outs_p, *outs_s)
```

```python
import functools

import numpy as np
import jax
import jax.numpy as jnp
from jax import lax
from jax.experimental import pallas as pl
from jax.experimental.pallas import tpu as pltpu

F32 = jnp.float32
BF16 = jnp.bfloat16
HIGHEST = lax.Precision.HIGHEST

D_MODEL = 1024
DEPTH = 4
DILATIONS = (1, 4, 16)
WINDOWS = (128, 512, 2048)
N_GROUPS = 3
HEADS_PER_GROUP = 4
HEAD_DIM_A = 64
GROUP_W = HEADS_PER_GROUP * HEAD_DIM_A
WIDTH_A = N_GROUPS * GROUP_W
N_HEADS_A = N_GROUPS * HEADS_PER_GROUP
STEPS_BACK = 128
A_BLOCK = 128
N_HEADS_B = 6
HEAD_DIM_B = 128
WIDTH_B = N_HEADS_B * HEAD_DIM_B
CONV_WIDTH = 4
CONV_DIM = 3 * WIDTH_B
DEEPNORM_ALPHA = (2 * DEPTH) ** 0.25
LN_EPS = 1e-5
RMS_EPS = 1e-6
LANES = 128
SUBLANES = 8
HALVES = GROUP_W // LANES

SRC_ZA = 3 * WIDTH_A
SRC_QKVB = SRC_ZA + WIDTH_A
SRC_ZB = SRC_QKVB + CONV_DIM
SRC_AB = SRC_ZB + WIDTH_B
SRC_GA = SRC_AB + 2 * N_HEADS_B
SRC_GB = SRC_GA + D_MODEL

ATT_W = 3 * WIDTH_A
ATT_BLOCKS = ATT_W // LANES
REST_ZA = 0
REST_QKVB = 768
REST_ZB = 3072
REST_AB = 3840
REST_GA = 4096
REST_GB = 5120
REST_W = 6144
REST_TILE = 2048

PROMPT_GDN_CHUNK = 128
TRI_BASE = 16
VMEM_LIMIT = 48 * 1024 * 1024

NT_DIMS = (((1,), (1,)), ((), ()))


def _cparams(sem):
    return pltpu.CompilerParams(dimension_semantics=sem, vmem_limit_bytes=VMEM_LIMIT)


def _silu(x):
    return x * jax.nn.sigmoid(x)


def _bdot(a, b):
    return jnp.dot(a.astype(BF16), b.astype(BF16), preferred_element_type=F32)


def _bdot_nt(a, b):
    return lax.dot_general(a.astype(BF16), b.astype(BF16), NT_DIMS, preferred_element_type=F32)


def _mod_kernel(c_ref, w_ref, b_ref, o_ref):
    o_ref[0] = _bdot(_silu(c_ref[...]), w_ref[0]) + b_ref[0]


def _modulation(c_all, w_ada, b_ada):
    n = c_all.shape[0]
    return pl.pallas_call(
        _mod_kernel,
        out_shape=jax.ShapeDtypeStruct((DEPTH, n, 3 * D_MODEL), F32),
        grid=(DEPTH, 3),
        in_specs=[
            pl.BlockSpec((n, D_MODEL), lambda l, j: (0, 0)),
            pl.BlockSpec((1, D_MODEL, D_MODEL), lambda l, j: (l, 0, j)),
            pl.BlockSpec((1, 1, D_MODEL), lambda l, j: (l, 0, j)),
        ],
        out_specs=pl.BlockSpec((1, n, D_MODEL), lambda l, j: (l, 0, j)),
        compiler_params=_cparams(("arbitrary", "arbitrary")),
        name="adaln_modulation",
    )(c_all, w_ada, b_ada.reshape(DEPTH, 1, 3 * D_MODEL))


def _modulated(x_ref, mod_ref):
    bs, bt, _ = x_ref.shape
    shift = mod_ref[:, :, 0:D_MODEL]
    scale = mod_ref[:, :, D_MODEL:2 * D_MODEL]
    h = x_ref[...] * (1.0 + scale) + shift
    return h.reshape(bs * bt, D_MODEL).astype(BF16)


def _inproj_att_kernel(x_ref, mod_ref, w_ref, o_ref):
    bs, bt, _ = x_ref.shape
    res = jnp.dot(_modulated(x_ref, mod_ref), w_ref[...], preferred_element_type=F32)
    for j in range(ATT_BLOCKS):
        o_ref[:, j] = res[:, j * LANES:(j + 1) * LANES].reshape(bs, bt, LANES)


def _inproj_rest_kernel(x_ref, mod_ref, w_ref, o_ref):
    bs, bt, _ = x_ref.shape
    res = jnp.dot(_modulated(x_ref, mod_ref), w_ref[...], preferred_element_type=F32)
    o_ref[...] = res.reshape(bs, bt, REST_TILE)


def _in_projection(x, mod, w_att, w_rest, bs, bt):
    nseq, t, _ = x.shape
    att = pl.pallas_call(
        _inproj_att_kernel,
        out_shape=jax.ShapeDtypeStruct((nseq, ATT_BLOCKS, t, LANES), F32),
        grid=(nseq // bs, t // bt),
        in_specs=[
            pl.BlockSpec((bs, bt, D_MODEL), lambda i, k: (i, k, 0)),
            pl.BlockSpec((bs, 1, 3 * D_MODEL), lambda i, k: (i, 0, 0)),
            pl.BlockSpec((D_MODEL, ATT_W), lambda i, k: (0, 0)),
        ],
        out_specs=pl.BlockSpec((bs, ATT_BLOCKS, bt, LANES), lambda i, k: (i, 0, k, 0)),
        compiler_params=_cparams(("arbitrary", "arbitrary")),
        name="in_projection_att",
    )(x, mod, w_att)
    rest = pl.pallas_call(
        _inproj_rest_kernel,
        out_shape=jax.ShapeDtypeStruct((nseq, t, REST_W), F32),
        grid=(REST_W // REST_TILE, nseq // bs, t // bt),
        in_specs=[
            pl.BlockSpec((bs, bt, D_MODEL), lambda j, i, k: (i, k, 0)),
            pl.BlockSpec((bs, 1, 3 * D_MODEL), lambda j, i, k: (i, 0, 0)),
            pl.BlockSpec((D_MODEL, REST_TILE), lambda j, i, k: (0, j)),
        ],
        out_specs=pl.BlockSpec((bs, bt, REST_TILE), lambda j, i, k: (i, k, j)),
        compiler_params=_cparams(("arbitrary", "arbitrary", "arbitrary")),
        name="in_projection_rest",
    )(x, mod, w_rest)
    return att, rest


def _attn_prompt_kernel(slopes_ref, q_ref, kp_ref, ko_ref, vp_ref, vo_ref, o_ref, lse_ref, *, group, dilation, units):
    n = pl.program_id(1)
    span = A_BLOCK * dilation
    qi = lax.broadcasted_iota(jnp.int32, (A_BLOCK, 2 * A_BLOCK), 0)
    kj = lax.broadcasted_iota(jnp.int32, (A_BLOCK, 2 * A_BLOCK), 1)
    steps = A_BLOCK + qi - kj
    band = (steps >= 0) & (steps <= STEPS_BACK)
    dist = (steps * dilation).astype(F32)
    lane_head = lax.broadcasted_iota(jnp.int32, (1, GROUP_W), 1) // HEAD_DIM_A

    def rows_of(ref, rows):
        return jnp.concatenate([ref[0, half, rows, :] for half in range(HALVES)], axis=-1)

    for u in range(units):
        if u == 0:
            valid = band & ((kj >= A_BLOCK) | (n > 0))
            prev_k, prev_v, prev_base = kp_ref, vp_ref, 0
        else:
            valid = band
            prev_k, prev_v, prev_base = ko_ref, vo_ref, (u - 1) * span

        def residue(r, carry, u=u, valid=valid, prev_k=prev_k, prev_v=prev_v, prev_base=prev_base):
            own = pl.ds(u * span + r, A_BLOCK, stride=dilation)
            prev = pl.ds(prev_base + r, A_BLOCK, stride=dilation)
            q = rows_of(q_ref, own) * (HEAD_DIM_A ** -0.5)
            k = jnp.concatenate([rows_of(prev_k, prev), rows_of(ko_ref, own)], axis=0).astype(BF16)
            v = jnp.concatenate([rows_of(prev_v, prev), rows_of(vo_ref, own)], axis=0).astype(BF16)
            o_acc = jnp.zeros((A_BLOCK, GROUP_W), F32)
            lse_acc = jnp.zeros((A_BLOCK, GROUP_W), F32)
            for h in range(HEADS_PER_GROUP):
                head = lane_head == h
                qh = jnp.where(head, q, 0.0).astype(BF16)
                s = lax.dot_general(qh, k, NT_DIMS, preferred_element_type=F32)
                s = jnp.where(valid, s - slopes_ref[group, h] * dist, -jnp.inf)
                m = s.max(-1, keepdims=True)
                p = jnp.exp(s - m)
                den = p.sum(-1, keepdims=True)
                oh = jnp.dot((p / den).astype(BF16), v, preferred_element_type=F32)
                o_acc = jnp.where(head, oh, o_acc)
                lse_acc = jnp.where(head, m + jnp.log(den), lse_acc)
            for half in range(HALVES):
                o_ref[0, half, own, :] = o_acc[:, half * LANES:(half + 1) * LANES]
                lse_ref[0, half, own, :] = lse_acc[:, half * LANES:(half + 1) * LANES]
            return carry

        if dilation == 1:
            residue(0, 0)
        else:
            lax.fori_loop(0, dilation, residue, 0)


def _attention_prompt(att, slopes, group):
    b, _, s, _ = att.shape
    d = DILATIONS[group]
    span = A_BLOCK * d
    units = max(1, 512 // span)
    rows = span * units
    qb, kb, vb = group, N_GROUPS + group, 2 * N_GROUPS + group
    own = lambda col: pl.BlockSpec((1, HALVES, rows, LANES), lambda i, n: (i, col, n, 0))
    prev = lambda col: pl.BlockSpec((1, HALVES, span, LANES), lambda i, n: (i, col, jnp.maximum(n * units - 1, 0), 0))
    out_spec = pl.BlockSpec((1, HALVES, rows, LANES), lambda i, n: (i, 0, n, 0))
    return pl.pallas_call(
        functools.partial(_attn_prompt_kernel, group=group, dilation=d, units=units),
        out_shape=[jax.ShapeDtypeStruct((b, HALVES, s, LANES), F32)] * 2,
        grid=(b, s // rows),
        in_specs=[pl.BlockSpec(memory_space=pltpu.SMEM), own(qb), prev(kb), own(kb), prev(vb), own(vb)],
        out_specs=[out_spec, out_spec],
        compiler_params=_cparams(("arbitrary", "arbitrary")),
        name=f"attn_prompt_g{group}",
    )(slopes, att, att, att, att, att)


def _sample_bias(slopes, group, t_new):
    d = DILATIONS[group]
    lb = WINDOWS[group]
    t = np.arange(t_new)

    def table(key_pos, width):
        diff = (lb + t)[:, None] - key_pos[None, :]
        ok = (diff >= 0) & (diff % d == 0) & (diff // d <= STEPS_BACK)
        dist = jnp.asarray(np.where(ok, diff, 0), F32)
        bias = jnp.where(jnp.asarray(ok)[None], -slopes[group][:, None, None] * dist[None], -jnp.inf)
        bias = bias.reshape(HEADS_PER_GROUP * t_new, key_pos.shape[0])
        return jnp.pad(bias, ((0, 0), (0, width - key_pos.shape[0])), constant_values=-jnp.inf)

    return table(np.arange(lb), lb), table(lb + t, LANES)


def _attn_sample_kernel(q_ref, k_ref, v_ref, cache_ref, bias_ref, biasn_ref, o_ref, lse_ref, *, bb, t_new):
    nq = HEADS_PER_GROUP * t_new
    row_head = lax.broadcasted_iota(jnp.int32, (nq, GROUP_W), 0) // t_new
    lane_head2 = lax.broadcasted_iota(jnp.int32, (nq, GROUP_W), 1) // HEAD_DIM_A
    lane_head = lax.broadcasted_iota(jnp.int32, (1, GROUP_W), 1) // HEAD_DIM_A
    pad_n = jnp.zeros((LANES - t_new, GROUP_W), F32)

    def new_rows(ref, i):
        return jnp.concatenate([ref[i, half] for half in range(HALVES)], axis=-1)

    def body(i, carry):
        q = new_rows(q_ref, i) * (HEAD_DIM_A ** -0.5)
        qm = jnp.where(row_head == lane_head2, jnp.concatenate([q] * HEADS_PER_GROUP, axis=0), 0.0).astype(BF16)
        kt = cache_ref[0, i, 0].astype(BF16)
        vt = cache_ref[0, i, 1].astype(BF16)
        kn = jnp.concatenate([new_rows(k_ref, i), pad_n], axis=0).astype(BF16)
        vn = jnp.concatenate([new_rows(v_ref, i), pad_n], axis=0).astype(BF16)
        sc = jnp.dot(qm, kt, preferred_element_type=F32) + bias_ref[...]
        sn = lax.dot_general(qm, kn, NT_DIMS, preferred_element_type=F32) + biasn_ref[...]
        m = jnp.maximum(sc.max(-1, keepdims=True), sn.max(-1, keepdims=True))
        pc = jnp.exp(sc - m)
        pn = jnp.exp(sn - m)
        den = pc.sum(-1, keepdims=True) + pn.sum(-1, keepdims=True)
        of = (lax.dot_general((pc / den).astype(BF16), vt, NT_DIMS, preferred_element_type=F32)
              + jnp.dot((pn / den).astype(BF16), vn, preferred_element_type=F32))
        lse_col = m + jnp.log(den)
        o = jnp.zeros((t_new, GROUP_W), F32)
        lse = jnp.zeros((t_new, GROUP_W), F32)
        for h in range(HEADS_PER_GROUP):
            head = lane_head == h
            o = jnp.where(head, of[h * t_new:(h + 1) * t_new, :], o)
            lse = jnp.where(head, lse_col[h * t_new:(h + 1) * t_new, :], lse)
        for half in range(HALVES):
            o_ref[i, half] = o[:, half * LANES:(half + 1) * LANES]
            lse_ref[i, half] = lse[:, half * LANES:(half + 1) * LANES]
        return carry

    lax.fori_loop(0, bb, body, 0)


def _attention_sample(att, cache_t, layer, slopes, group, bb):
    nseq, _, t_new, _ = att.shape
    lb = WINDOWS[group]
    bias, bias_new = _sample_bias(slopes, group, t_new)
    qb, kb, vb = group, N_GROUPS + group, 2 * N_GROUPS + group
    new_spec = lambda col: pl.BlockSpec((bb, HALVES, t_new, LANES), lambda i: (i, col, 0, 0))
    out_spec = pl.BlockSpec((bb, HALVES, t_new, LANES), lambda i: (i, 0, 0, 0))
    return pl.pallas_call(
        functools.partial(_attn_sample_kernel, bb=bb, t_new=t_new),
        out_shape=[jax.ShapeDtypeStruct((nseq, HALVES, t_new, LANES), F32)] * 2,
        grid=(nseq // bb,),
        in_specs=[new_spec(qb), new_spec(kb), new_spec(vb),
                  pl.BlockSpec((1, bb, 2, GROUP_W, lb), lambda i: (layer, i, 0, 0, 0)),
                  pl.BlockSpec(bias.shape, lambda i: (0, 0)), pl.BlockSpec(bias_new.shape, lambda i: (0, 0))],
        out_specs=[out_spec, out_spec],
        compiler_params=_cparams(("arbitrary",)),
        name=f"attn_sample_g{group}",
    )(att, att, att, cache_t, bias, bias_new)


def _unit_lower_inverse(lower, ri, ci):
    c = lower.shape[0]
    base = min(TRI_BASE, c)
    ident = (ri == ci).astype(F32)
    diag = jnp.where((ri // base) == (ci // base), lower, 0.0)
    inv = ident - diag
    power = diag
    for _ in range(int(np.log2(base)) - 1):
        power = _bdot(power, power)
        inv = inv + _bdot(inv, power)
    size = base
    while size < c:
        pair = ((ri // (2 * size)) == (ci // (2 * size))) & ((ri // size) % 2 == 1) & ((ci // size) % 2 == 0)
        off = jnp.where(pair, lower, 0.0)
        inv = inv - _bdot(inv, _bdot(off, inv))
        size *= 2
    return inv


def _gdn_kernel(q_ref, k_ref, v_ref, ab_ref, conv0_ref, s0_ref, cw_ref, par_ref, o_ref, sout_ref, ext_ref, s_ref,
                *, bb, chunk, layered_state):
    c_idx = pl.program_id(1)
    C = chunk
    first_row = SUBLANES - (CONV_WIDTH - 1)

    @pl.when(c_idx == 0)
    def _():
        ext_ref[:, 0:SUBLANES, :] = conv0_ref[...]
        s_ref[...] = s0_ref[0] if layered_state else s0_ref[...]

    @pl.when(c_idx > 0)
    def _():
        ext_ref[:, 0:SUBLANES, :] = ext_ref[:, C:C + SUBLANES, :]

    ext_ref[:, SUBLANES:SUBLANES + C, 0:WIDTH_B] = q_ref[...]
    ext_ref[:, SUBLANES:SUBLANES + C, WIDTH_B:2 * WIDTH_B] = k_ref[...]
    ext_ref[:, SUBLANES:SUBLANES + C, 2 * WIDTH_B:3 * WIDTH_B] = v_ref[...]

    ri = lax.broadcasted_iota(jnp.int32, (C, C), 0)
    ci = lax.broadcasted_iota(jnp.int32, (C, C), 1)
    causal = ri >= ci
    strict = ri > ci
    tril = causal.astype(F32)
    neg_a = -jnp.exp(par_ref[0:1, :])
    dt_bias = par_ref[1:2, :]
    norm_w = par_ref[2:3, :]
    pad_rows = jnp.zeros((HEAD_DIM_B - C, HEAD_DIM_B), F32) if C < HEAD_DIM_B else None

    def pad128(x):
        return x if pad_rows is None else jnp.concatenate([x, pad_rows], axis=0)

    def conv(i, col):
        acc = ext_ref[i, pl.ds(first_row, C), pl.ds(col, HEAD_DIM_B)] * cw_ref[0:1, pl.ds(col, HEAD_DIM_B)]
        for j in range(1, CONV_WIDTH):
            acc = acc + (ext_ref[i, pl.ds(first_row + j, C), pl.ds(col, HEAD_DIM_B)]
                         * cw_ref[j:j + 1, pl.ds(col, HEAD_DIM_B)])
        return _silu(acc)

    def l2n(x):
        return x * lax.rsqrt(jnp.sum(x * x, axis=-1, keepdims=True) + RMS_EPS)

    def body(i, carry):
        ab = ab_ref[i]
        z = ab + dt_bias
        softplus = jnp.maximum(z, 0.0) + jnp.log(1.0 + jnp.exp(-jnp.abs(z)))
        g_all = neg_a * softplus
        beta_all = jax.nn.sigmoid(ab)
        gcum_all = jnp.dot(tril, g_all, preferred_element_type=F32, precision=HIGHEST)
        gcum_rows = pad128(gcum_all).T
        for h in range(N_HEADS_B):
            col = h * HEAD_DIM_B
            q = l2n(conv(i, col)) * (HEAD_DIM_B ** -0.5)
            k = l2n(conv(i, WIDTH_B + col))
            v = conv(i, 2 * WIDTH_B + col)
            beta = beta_all[:, N_HEADS_B + h:N_HEADS_B + h + 1]
            gcum = gcum_all[:, h:h + 1]
            grow = gcum_rows[h:h + 1, 0:C]
            decay = jnp.where(causal, jnp.exp(jnp.minimum(gcum - grow, 0.0)), 0.0)
            qk = _bdot_nt(jnp.concatenate([q, k], axis=0), k)
            intra = qk[0:C] * decay
            lower = jnp.where(strict, qk[C:2 * C] * beta * decay, 0.0)
            inv = _unit_lower_inverse(lower, ri, ci)
            e_g = jnp.exp(gcum)
            kbeta = k * beta
            uw = _bdot(inv, jnp.concatenate([v * beta, kbeta * e_g], axis=-1))
            u = uw[:, 0:HEAD_DIM_B]
            w = uw[:, HEAD_DIM_B:2 * HEAD_DIM_B]
            state = s_ref[i, h]
            ws_qs = _bdot(jnp.concatenate([w, q * e_g], axis=0), state)
            v_new = u - ws_qs[0:C]
            o = ws_qs[C:2 * C] + _bdot(intra, v_new)
            g_last = gcum[C - 1:C, :]
            k_dec = k * jnp.exp(g_last - gcum)
            s_ref[i, h] = state * jnp.exp(g_last) + _bdot(pad128(k_dec).T, pad128(v_new))
            o = o * lax.rsqrt(jnp.mean(o * o, axis=-1, keepdims=True) + RMS_EPS) * norm_w
            o_ref[i, :, pl.ds(col, HEAD_DIM_B)] = o
        return carry

    lax.fori_loop(0, bb, body, 0)

    @pl.when(c_idx == pl.num_programs(1) - 1)
    def _():
        sout_ref[...] = s_ref[...]


def _gated_deltanet(rest, conv0, s0, layer, conv_w, params, bb, chunk):
    nseq, t, _ = rest.shape
    qb = REST_QKVB // WIDTH_B
    col_spec = lambda col: pl.BlockSpec((bb, chunk, WIDTH_B), lambda i, c: (i, c, col))
    state_shape = (bb, N_HEADS_B, HEAD_DIM_B, HEAD_DIM_B)
    state_spec = pl.BlockSpec(state_shape, lambda i, c: (i, 0, 0, 0))
    layered = s0.ndim == 5
    s0_spec = pl.BlockSpec((1,) + state_shape, lambda i, c: (layer, i, 0, 0, 0)) if layered else state_spec
    return pl.pallas_call(
        functools.partial(_gdn_kernel, bb=bb, chunk=chunk, layered_state=layered),
        out_shape=[jax.ShapeDtypeStruct((nseq, t, WIDTH_B), F32),
                   jax.ShapeDtypeStruct((nseq, N_HEADS_B, HEAD_DIM_B, HEAD_DIM_B), F32)],
        grid=(nseq // bb, t // chunk),
        in_specs=[col_spec(qb), col_spec(qb + 1), col_spec(qb + 2),
                  pl.BlockSpec((bb, chunk, LANES), lambda i, c: (i, c, REST_AB // LANES)),
                  pl.BlockSpec((bb, SUBLANES, CONV_DIM), lambda i, c: (i, 0, 0)),
                  s0_spec,
                  pl.BlockSpec((SUBLANES, CONV_DIM), lambda i, c: (0, 0)),
                  pl.BlockSpec((SUBLANES, LANES), lambda i, c: (0, 0))],
        out_specs=[pl.BlockSpec((bb, chunk, WIDTH_B), lambda i, c: (i, c, 0)), state_spec],
        scratch_shapes=[pltpu.VMEM((bb, chunk + SUBLANES, CONV_DIM), F32),
                        pltpu.VMEM(state_shape, F32)],
        compiler_params=_cparams(("arbitrary", "arbitrary")),
        name="gated_deltanet",
    )(rest, rest, rest, rest, conv0, s0, conv_w, params)


def _out_kernel(x_ref, mod_ref, o1_ref, o2_ref, o3_ref, l1_ref, l2_ref, l3_ref, za_ref, ob_ref, zb_ref, ga_ref, gb_ref,
                wa_ref, wb_ref, wo_ref, lng_ref, lnb_ref, y_ref):
    bs, bt, _ = x_ref.shape
    rows = bs * bt
    flat = lambda ref: ref[...].reshape(rows, ref.shape[-1])
    group = lambda ref: jnp.concatenate([ref[:, half].reshape(rows, LANES) for half in range(HALVES)], axis=-1)
    l1, l2, l3 = group(l1_ref), group(l2_ref), group(l3_ref)
    m = jnp.maximum(jnp.maximum(l1, l2), l3)
    e1, e2, e3 = jnp.exp(l1 - m), jnp.exp(l2 - m), jnp.exp(l3 - m)
    tot = e1 + e2 + e3
    att = jnp.concatenate([group(o1_ref) * (e1 / tot), group(o2_ref) * (e2 / tot), group(o3_ref) * (e3 / tot)], axis=-1)
    branch_a = jnp.dot((att * _silu(flat(za_ref))).astype(BF16), wa_ref[...], preferred_element_type=F32)
    branch_b = jnp.dot((flat(ob_ref) * _silu(flat(zb_ref))).astype(BF16), wb_ref[...], preferred_element_type=F32)
    merged = jax.nn.sigmoid(flat(ga_ref)) * branch_a + jax.nn.sigmoid(flat(gb_ref)) * branch_b
    out = jnp.dot(merged.astype(BF16), wo_ref[...], preferred_element_type=F32).reshape(bs, bt, D_MODEL)
    gate = mod_ref[:, :, 2 * D_MODEL:3 * D_MODEL]
    y = DEEPNORM_ALPHA * x_ref[...] + gate * out
    mu = jnp.mean(y, axis=-1, keepdims=True)
    yc = y - mu
    var = jnp.mean(yc * yc, axis=-1, keepdims=True)
    y_ref[...] = yc * lax.rsqrt(var + LN_EPS) * lng_ref[...] + lnb_ref[...]


def _output_block(x, mod, o_g, lse_g, o_b, rest, wa, wb, wo, ln_g, ln_b, bs, bt):
    nseq, t, _ = x.shape
    row = lambda w, col: pl.BlockSpec((bs, bt, w), lambda i, k: (i, k, col))
    halves = pl.BlockSpec((bs, HALVES, bt, LANES), lambda i, k: (i, 0, k, 0))
    const = lambda shape: pl.BlockSpec(shape, lambda i, k: (0,) * len(shape))
    return pl.pallas_call(
        _out_kernel,
        out_shape=jax.ShapeDtypeStruct((nseq, t, D_MODEL), F32),
        grid=(nseq // bs, t // bt),
        in_specs=[row(D_MODEL, 0),
                  pl.BlockSpec((bs, 1, 3 * D_MODEL), lambda i, k: (i, 0, 0)),
                  halves, halves, halves, halves, halves, halves,
                  row(WIDTH_A, REST_ZA // WIDTH_A),
                  row(WIDTH_B, 0),
                  row(WIDTH_B, REST_ZB // WIDTH_B),
                  row(D_MODEL, REST_GA // D_MODEL),
                  row(D_MODEL, REST_GB // D_MODEL),
                  const((WIDTH_A, D_MODEL)), const((WIDTH_B, D_MODEL)), const((D_MODEL, D_MODEL)),
                  const((1, 1, D_MODEL)), const((1, 1, D_MODEL))],
        out_specs=row(D_MODEL, 0),
        compiler_params=_cparams(("arbitrary", "arbitrary")),
        name="output_block",
    )(x, mod, o_g[0], o_g[1], o_g[2], lse_g[0], lse_g[1], lse_g[2], rest, o_b, rest, rest, rest,
      wa, wb, wo, ln_g.reshape(1, 1, D_MODEL), ln_b.reshape(1, 1, D_MODEL))


def _alibi_slopes():
    i = jnp.arange(1, N_HEADS_A + 1, dtype=F32)
    return jnp.exp2(-8.0 * i / N_HEADS_A).reshape(N_GROUPS, HEADS_PER_GROUP)


def _split_w_in(w_in):
    att = w_in[:, :, :ATT_W]
    pad = jnp.zeros(w_in.shape[:2] + (REST_GA - REST_AB - 2 * N_HEADS_B,), w_in.dtype)
    rest = jnp.concatenate([w_in[:, :, SRC_ZA:SRC_AB], w_in[:, :, SRC_AB:SRC_GA], pad, w_in[:, :, SRC_GA:]], axis=-1)
    return att.astype(BF16), rest.astype(BF16)


def _split_kv(att, group, start):
    b = att.shape[0]
    k = att[:, (N_GROUPS + group) * HALVES:(N_GROUPS + group + 1) * HALVES, start:, :]
    v = att[:, (2 * N_GROUPS + group) * HALVES:(2 * N_GROUPS + group + 1) * HALVES, start:, :]
    kv = jnp.stack([k, v], axis=1)
    kv = kv.transpose(0, 3, 1, 2, 4)
    return kv.reshape(b, kv.shape[1], 2, HEADS_PER_GROUP, HEAD_DIM_A)


def _feature_major(cache):
    d, b, l = cache.shape[:3]
    return cache.transpose(0, 1, 3, 4, 5, 2).reshape(d, b, 2, GROUP_W, l)


def kernel(x_prompt, x_sample, c_prompt, c_sample, cache_win1, cache_win2, cache_win3, state_gdn, state_conv,
           w_ada, b_ada, w_in, conv_w, a_log, dt_bias, gdn_norm, w_out_a, w_out_b, w_o, ln_g, ln_b):
    n_p, seq, _ = x_prompt.shape
    n_s, t_new, _ = x_sample.shape
    caches = tuple(_feature_major(c) for c in (cache_win1, cache_win2, cache_win3))
    sample_attn_bb = (8, 8, 2)
    slopes = _alibi_slopes()

    w_att, w_rest = _split_w_in(w_in)
    wa = w_out_a.astype(BF16)
    wb = w_out_b.astype(BF16)
    wo = w_o.astype(BF16)
    conv_w_p = jnp.pad(conv_w, ((0, 0), (0, SUBLANES - CONV_WIDTH), (0, 0)))
    params = jnp.zeros((DEPTH, SUBLANES, LANES), F32)
    params = params.at[:, 0, :N_HEADS_B].set(a_log).at[:, 1, :N_HEADS_B].set(dt_bias).at[:, 2, :].set(gdn_norm)

    n_all = n_p + n_s
    n_pad = -(-n_all // SUBLANES) * SUBLANES
    c_all = jnp.pad(jnp.concatenate([c_prompt, c_sample], axis=0), ((0, n_pad - n_all), (0, 0)))
    mod_all = _modulation(c_all, w_ada, b_ada)

    zero_conv = jnp.zeros((n_p, SUBLANES, CONV_DIM), F32)
    zero_state = jnp.zeros((n_p, N_HEADS_B, HEAD_DIM_B, HEAD_DIM_B), F32)
    conv_s = jnp.pad(state_conv, ((0, 0), (0, 0), (SUBLANES - (CONV_WIDTH - 1), 0), (0, 0)))

    xp, xs = x_prompt, x_sample
    acc_p = [[] for _ in range(N_GROUPS + 2)]
    acc_s = [[] for _ in range(N_GROUPS + 2)]
    for l in range(DEPTH):
        mod_p = mod_all[l, :n_p, None, :]
        mod_s = mod_all[l, n_p:n_all, None, :]

        att_p, rest_p = _in_projection(xp, mod_p, w_att[l], w_rest[l], 1, 512)
        att_s, rest_s = _in_projection(xs, mod_s, w_att[l], w_rest[l], 32, t_new)

        o_p, lse_p, o_s, lse_s = [], [], [], []
        for g in range(N_GROUPS):
            o, lse = _attention_prompt(att_p, slopes, g)
            o_p.append(o)
            lse_p.append(lse)
            o, lse = _attention_sample(att_s, caches[g], l, slopes, g, sample_attn_bb[g])
            o_s.append(o)
            lse_s.append(lse)

        ob_p, st_p = _gated_deltanet(rest_p, zero_conv, zero_state, 0, conv_w_p[l], params[l], 1, PROMPT_GDN_CHUNK)
        ob_s, st_s = _gated_deltanet(rest_s, conv_s[l], state_gdn, l, conv_w_p[l], params[l], 8, t_new)

        xp_new = _output_block(xp, mod_p, o_p, lse_p, ob_p, rest_p, wa[l], wb[l], wo[l], ln_g[l], ln_b[l], 1, 256)
        xs_new = _output_block(xs, mod_s, o_s, lse_s, ob_s, rest_s, wa[l], wb[l], wo[l], ln_g[l], ln_b[l], 32, t_new)

        for g in range(N_GROUPS):
            acc_p[g].append(_split_kv(att_p, g, seq - min(WINDOWS[g], seq)))
            acc_s[g].append(_split_kv(att_s, g, 0))
        acc_p[N_GROUPS].append(st_p)
        acc_s[N_GROUPS].append(st_s)
        acc_p[N_GROUPS + 1].append(rest_p[:, seq - (CONV_WIDTH - 1):, REST_QKVB:REST_QKVB + CONV_DIM])
        acc_s[N_GROUPS + 1].append(rest_s[:, t_new - (CONV_WIDTH - 1):, REST_QKVB:REST_QKVB + CONV_DIM])
        xp, xs = xp_new, xs_new

    outs_p = [jnp.stack(a) for a in acc_p]
    outs_s = [jnp.stack(a) for a in acc_s]
    return (xp, xs, *outs_p, *outs_s)
```

```python
import functools

import numpy as np
import jax
import jax.numpy as jnp
from jax import lax
from jax.experimental import pallas as pl
from jax.experimental.pallas import tpu as pltpu

F32 = jnp.float32
BF16 = jnp.bfloat16
HIGHEST = lax.Precision.HIGHEST

D_MODEL = 1024
DEPTH = 4
DILATIONS = (1, 4, 16)
WINDOWS = (128, 512, 2048)
N_GROUPS = 3
HEADS_PER_GROUP = 4
HEAD_DIM_A = 64
GROUP_W = HEADS_PER_GROUP * HEAD_DIM_A
WIDTH_A = N_GROUPS * GROUP_W
N_HEADS_A = N_GROUPS * HEADS_PER_GROUP
STEPS_BACK = 128
A_BLOCK = 128
N_HEADS_B = 6
HEAD_DIM_B = 128
WIDTH_B = N_HEADS_B * HEAD_DIM_B
CONV_WIDTH = 4
CONV_DIM = 3 * WIDTH_B
DEEPNORM_ALPHA = (2 * DEPTH) ** 0.25
LN_EPS = 1e-5
RMS_EPS = 1e-6
LANES = 128
SUBLANES = 8
HALVES = GROUP_W // LANES

SRC_ZA = 3 * WIDTH_A
SRC_QKVB = SRC_ZA + WIDTH_A
SRC_ZB = SRC_QKVB + CONV_DIM
SRC_AB = SRC_ZB + WIDTH_B
SRC_GA = SRC_AB + 2 * N_HEADS_B
SRC_GB = SRC_GA + D_MODEL

ATT_W = 3 * WIDTH_A
ATT_BLOCKS = ATT_W // LANES
REST_ZA = 0
REST_QKVB = 768
REST_ZB = 3072
REST_AB = 3840
REST_GA = 4096
REST_GB = 5120
REST_W = 6144
REST_TILE = 2048

PROMPT_GDN_CHUNK = 128
TRI_BASE = 16
ATTN_TASKS = 2
ATTN_STATIC_BLOCKS = 4
VMEM_LIMIT = 48 * 1024 * 1024

NT_DIMS = (((1,), (1,)), ((), ()))


def _cparams(sem):
    return pltpu.CompilerParams(dimension_semantics=sem, vmem_limit_bytes=VMEM_LIMIT)


def _sigmoid(x):
    return 0.5 * jnp.tanh(0.5 * x) + 0.5


def _silu(x):
    return x * _sigmoid(x)


def _bdot(a, b):
    return jnp.dot(a.astype(BF16), b.astype(BF16), preferred_element_type=F32)


def _bdot_nt(a, b):
    return lax.dot_general(a.astype(BF16), b.astype(BF16), NT_DIMS, preferred_element_type=F32)


def _mod_kernel(c_ref, w_ref, b_ref, o_ref):
    o_ref[0] = _bdot(_silu(c_ref[...]), w_ref[0]) + b_ref[0]


def _modulation(c_all, w_ada, b_ada):
    n = c_all.shape[0]
    return pl.pallas_call(
        _mod_kernel,
        out_shape=jax.ShapeDtypeStruct((DEPTH, n, 3 * D_MODEL), F32),
        grid=(DEPTH, 3),
        in_specs=[
            pl.BlockSpec((n, D_MODEL), lambda l, j: (0, 0)),
            pl.BlockSpec((1, D_MODEL, D_MODEL), lambda l, j: (l, 0, j)),
            pl.BlockSpec((1, 1, D_MODEL), lambda l, j: (l, 0, j)),
        ],
        out_specs=pl.BlockSpec((1, n, D_MODEL), lambda l, j: (l, 0, j)),
        compiler_params=_cparams(("arbitrary", "arbitrary")),
        name="adaln_modulation",
    )(c_all, w_ada, b_ada.reshape(DEPTH, 1, 3 * D_MODEL))


def _modulated(x_ref, mod_ref):
    bs, bt, _ = x_ref.shape
    shift = mod_ref[:, :, 0:D_MODEL]
    scale = mod_ref[:, :, D_MODEL:2 * D_MODEL]
    h = x_ref[...] * (1.0 + scale) + shift
    return h.reshape(bs * bt, D_MODEL).astype(BF16)


def _inproj_att_kernel(x_ref, mod_ref, w_ref, o_ref):
    bs, bt, _ = x_ref.shape
    res = jnp.dot(_modulated(x_ref, mod_ref), w_ref[...], preferred_element_type=F32)
    for j in range(ATT_BLOCKS):
        o_ref[:, j] = res[:, j * LANES:(j + 1) * LANES].reshape(bs, bt, LANES)


def _inproj_rest_kernel(x_ref, mod_ref, w_ref, o_ref):
    bs, bt, _ = x_ref.shape
    res = jnp.dot(_modulated(x_ref, mod_ref), w_ref[...], preferred_element_type=F32)
    o_ref[...] = res.reshape(bs, bt, REST_TILE)


def _in_projection(x, mod, w_att, w_rest, bs, bt):
    nseq, t, _ = x.shape
    att = pl.pallas_call(
        _inproj_att_kernel,
        out_shape=jax.ShapeDtypeStruct((nseq, ATT_BLOCKS, t, LANES), F32),
        grid=(nseq // bs, t // bt),
        in_specs=[
            pl.BlockSpec((bs, bt, D_MODEL), lambda i, k: (i, k, 0)),
            pl.BlockSpec((bs, 1, 3 * D_MODEL), lambda i, k: (i, 0, 0)),
            pl.BlockSpec((D_MODEL, ATT_W), lambda i, k: (0, 0)),
        ],
        out_specs=pl.BlockSpec((bs, ATT_BLOCKS, bt, LANES), lambda i, k: (i, 0, k, 0)),
        compiler_params=_cparams(("arbitrary", "arbitrary")),
        name="in_projection_att",
    )(x, mod, w_att)
    rest = pl.pallas_call(
        _inproj_rest_kernel,
        out_shape=jax.ShapeDtypeStruct((nseq, t, REST_W), F32),
        grid=(REST_W // REST_TILE, nseq // bs, t // bt),
        in_specs=[
            pl.BlockSpec((bs, bt, D_MODEL), lambda j, i, k: (i, k, 0)),
            pl.BlockSpec((bs, 1, 3 * D_MODEL), lambda j, i, k: (i, 0, 0)),
            pl.BlockSpec((D_MODEL, REST_TILE), lambda j, i, k: (0, j)),
        ],
        out_specs=pl.BlockSpec((bs, bt, REST_TILE), lambda j, i, k: (i, k, j)),
        compiler_params=_cparams(("arbitrary", "arbitrary", "arbitrary")),
        name="in_projection_rest",
    )(x, mod, w_rest)
    return att, rest


def _attn_prompt_kernel(slopes_ref, q_ref, kp_ref, ko_ref, vp_ref, vo_ref, o_ref, lse_ref, *, group, dilation, units):
    n = pl.program_id(1)
    span = A_BLOCK * dilation
    qi = lax.broadcasted_iota(jnp.int32, (A_BLOCK, 2 * A_BLOCK), 0)
    kj = lax.broadcasted_iota(jnp.int32, (A_BLOCK, 2 * A_BLOCK), 1)
    steps = A_BLOCK + qi - kj
    band = (steps >= 0) & (steps <= STEPS_BACK)
    first_span = band & ((kj >= A_BLOCK) | (n > 0))
    dist = (steps * dilation).astype(F32)
    lane_head = lax.broadcasted_iota(jnp.int32, (1, GROUP_W), 1) // HEAD_DIM_A
    heads = range(HEADS_PER_GROUP)

    def rows_of(ref, rows):
        return jnp.concatenate([ref[0, half, rows, :] for half in range(HALVES)], axis=-1)

    def run(tasks):
        q, k, v, valid, own_rows = [], [], [], [], []
        for u, r in tasks:
            own = pl.ds(u * span + r, A_BLOCK, stride=dilation)
            if u == 0:
                prev_k, prev_v, prev = kp_ref, vp_ref, pl.ds(r, A_BLOCK, stride=dilation)
            else:
                prev_k, prev_v, prev = ko_ref, vo_ref, pl.ds((u - 1) * span + r, A_BLOCK, stride=dilation)
            own_rows.append(own)
            valid.append(first_span if u == 0 else band)
            q.append(rows_of(q_ref, own) * (HEAD_DIM_A ** -0.5))
            k.append(jnp.concatenate([rows_of(prev_k, prev), rows_of(ko_ref, own)], axis=0).astype(BF16))
            v.append(jnp.concatenate([rows_of(prev_v, prev), rows_of(vo_ref, own)], axis=0).astype(BF16))
        chains = [(t, h) for t in range(len(tasks)) for h in heads]
        s = [lax.dot_general(jnp.where(lane_head == h, q[t], 0.0).astype(BF16), k[t], NT_DIMS,
                             preferred_element_type=F32) for t, h in chains]
        s = [jnp.where(valid[t], s[c] - slopes_ref[group, h] * dist, -jnp.inf) for c, (t, h) in enumerate(chains)]
        m = [x.max(-1, keepdims=True) for x in s]
        p = [jnp.exp(x - mx) for x, mx in zip(s, m)]
        den = [x.sum(-1, keepdims=True) for x in p]
        oh = [jnp.dot((p[c] / den[c]).astype(BF16), v[t], preferred_element_type=F32) for c, (t, h) in enumerate(chains)]
        for t in range(len(tasks)):
            o_acc = jnp.zeros((A_BLOCK, GROUP_W), F32)
            lse_acc = jnp.zeros((A_BLOCK, GROUP_W), F32)
            for h in heads:
                c = t * HEADS_PER_GROUP + h
                o_acc = jnp.where(lane_head == h, oh[c], o_acc)
                lse_acc = jnp.where(lane_head == h, m[c] + jnp.log(den[c]), lse_acc)
            for half in range(HALVES):
                o_ref[0, half, own_rows[t], :] = o_acc[:, half * LANES:(half + 1) * LANES]
                lse_ref[0, half, own_rows[t], :] = lse_acc[:, half * LANES:(half + 1) * LANES]

    if dilation * units <= ATTN_STATIC_BLOCKS:
        blocks = [(u, r) for u in range(units) for r in range(dilation)]
        for j in range(0, len(blocks), ATTN_TASKS):
            run(blocks[j:j + ATTN_TASKS])
    else:
        for u in range(units):
            def residues(j, carry, u=u):
                run([(u, j * ATTN_TASKS + t) for t in range(ATTN_TASKS)])
                return carry
            lax.fori_loop(0, dilation // ATTN_TASKS, residues, 0)


def _attention_prompt(att, slopes, group):
    b, _, s, _ = att.shape
    d = DILATIONS[group]
    span = A_BLOCK * d
    units = max(1, 512 // span)
    rows = span * units
    qb, kb, vb = group, N_GROUPS + group, 2 * N_GROUPS + group
    own = lambda col: pl.BlockSpec((1, HALVES, rows, LANES), lambda i, n: (i, col, n, 0))
    prev = lambda col: pl.BlockSpec((1, HALVES, span, LANES), lambda i, n: (i, col, jnp.maximum(n * units - 1, 0), 0))
    out_spec = pl.BlockSpec((1, HALVES, rows, LANES), lambda i, n: (i, 0, n, 0))
    return pl.pallas_call(
        functools.partial(_attn_prompt_kernel, group=group, dilation=d, units=units),
        out_shape=[jax.ShapeDtypeStruct((b, HALVES, s, LANES), F32)] * 2,
        grid=(b, s // rows),
        in_specs=[pl.BlockSpec(memory_space=pltpu.SMEM), own(qb), prev(kb), own(kb), prev(vb), own(vb)],
        out_specs=[out_spec, out_spec],
        compiler_params=_cparams(("arbitrary", "arbitrary")),
        name=f"attn_prompt_g{group}",
    )(slopes, att, att, att, att, att)


def _sample_bias(slopes, group, t_new):
    d = DILATIONS[group]
    lb = WINDOWS[group]
    t = np.arange(t_new)

    def table(key_pos, width):
        diff = (lb + t)[:, None] - key_pos[None, :]
        ok = (diff >= 0) & (diff % d == 0) & (diff // d <= STEPS_BACK)
        dist = jnp.asarray(np.where(ok, diff, 0), F32)
        bias = jnp.where(jnp.asarray(ok)[None], -slopes[group][:, None, None] * dist[None], -jnp.inf)
        bias = bias.reshape(HEADS_PER_GROUP * t_new, key_pos.shape[0])
        return jnp.pad(bias, ((0, 0), (0, width - key_pos.shape[0])), constant_values=-jnp.inf)

    return table(np.arange(lb), lb), table(lb + t, LANES)


def _attn_sample_kernel(q_ref, k_ref, v_ref, cache_ref, bias_ref, biasn_ref, o_ref, lse_ref, *, bb, t_new):
    nq = HEADS_PER_GROUP * t_new
    row_head = lax.broadcasted_iota(jnp.int32, (nq, GROUP_W), 0) // t_new
    lane_head2 = lax.broadcasted_iota(jnp.int32, (nq, GROUP_W), 1) // HEAD_DIM_A
    lane_head = lax.broadcasted_iota(jnp.int32, (1, GROUP_W), 1) // HEAD_DIM_A
    pad_n = jnp.zeros((LANES - t_new, GROUP_W), F32)

    def new_rows(ref, i):
        return jnp.concatenate([ref[i, half] for half in range(HALVES)], axis=-1)

    def body(i, carry):
        q = new_rows(q_ref, i) * (HEAD_DIM_A ** -0.5)
        qm = jnp.where(row_head == lane_head2, jnp.concatenate([q] * HEADS_PER_GROUP, axis=0), 0.0).astype(BF16)
        kt = cache_ref[0, i, 0].astype(BF16)
        vt = cache_ref[0, i, 1].astype(BF16)
        kn = jnp.concatenate([new_rows(k_ref, i), pad_n], axis=0).astype(BF16)
        vn = jnp.concatenate([new_rows(v_ref, i), pad_n], axis=0).astype(BF16)
        sc = jnp.dot(qm, kt, preferred_element_type=F32) + bias_ref[...]
        sn = lax.dot_general(qm, kn, NT_DIMS, preferred_element_type=F32) + biasn_ref[...]
        m = jnp.maximum(sc.max(-1, keepdims=True), sn.max(-1, keepdims=True))
        pc = jnp.exp(sc - m)
        pn = jnp.exp(sn - m)
        den = pc.sum(-1, keepdims=True) + pn.sum(-1, keepdims=True)
        of = (lax.dot_general((pc / den).astype(BF16), vt, NT_DIMS, preferred_element_type=F32)
              + jnp.dot((pn / den).astype(BF16), vn, preferred_element_type=F32))
        lse_col = m + jnp.log(den)
        o = jnp.zeros((t_new, GROUP_W), F32)
        lse = jnp.zeros((t_new, GROUP_W), F32)
        for h in range(HEADS_PER_GROUP):
            head = lane_head == h
            o = jnp.where(head, of[h * t_new:(h + 1) * t_new, :], o)
            lse = jnp.where(head, lse_col[h * t_new:(h + 1) * t_new, :], lse)
        for half in range(HALVES):
            o_ref[i, half] = o[:, half * LANES:(half + 1) * LANES]
            lse_ref[i, half] = lse[:, half * LANES:(half + 1) * LANES]
        return carry

    lax.fori_loop(0, bb, body, 0)


def _attention_sample(att, cache_t, layer, slopes, group, bb):
    nseq, _, t_new, _ = att.shape
    lb = WINDOWS[group]
    bias, bias_new = _sample_bias(slopes, group, t_new)
    qb, kb, vb = group, N_GROUPS + group, 2 * N_GROUPS + group
    new_spec = lambda col: pl.BlockSpec((bb, HALVES, t_new, LANES), lambda i: (i, col, 0, 0))
    out_spec = pl.BlockSpec((bb, HALVES, t_new, LANES), lambda i: (i, 0, 0, 0))
    return pl.pallas_call(
        functools.partial(_attn_sample_kernel, bb=bb, t_new=t_new),
        out_shape=[jax.ShapeDtypeStruct((nseq, HALVES, t_new, LANES), F32)] * 2,
        grid=(nseq // bb,),
        in_specs=[new_spec(qb), new_spec(kb), new_spec(vb),
                  pl.BlockSpec((1, bb, 2, GROUP_W, lb), lambda i: (layer, i, 0, 0, 0)),
                  pl.BlockSpec(bias.shape, lambda i: (0, 0)), pl.BlockSpec(bias_new.shape, lambda i: (0, 0))],
        out_specs=[out_spec, out_spec],
        compiler_params=_cparams(("arbitrary",)),
        name=f"attn_sample_g{group}",
    )(att, att, att, cache_t, bias, bias_new)


def _unit_lower_inverses(lowers, ri, ci):
    c = lowers[0].shape[0]
    base = min(TRI_BASE, c)
    ident = (ri == ci).astype(F32)
    same = (ri // base) == (ci // base)
    powers = [jnp.where(same, l, 0.0) for l in lowers]
    invs = [ident - p for p in powers]
    for _ in range(int(np.log2(base)) - 1):
        powers = [_bdot(p, p) for p in powers]
        invs = [t + _bdot(t, p) for t, p in zip(invs, powers)]
    size = base
    while size < c:
        pair = ((ri // (2 * size)) == (ci // (2 * size))) & ((ri // size) % 2 == 1) & ((ci // size) % 2 == 0)
        offs = [_bdot(jnp.where(pair, l, 0.0), t) for l, t in zip(lowers, invs)]
        invs = [t - _bdot(t, o) for t, o in zip(invs, offs)]
        size *= 2
    return invs


def _gdn_kernel(q_ref, k_ref, v_ref, ab_ref, conv0_ref, s0_ref, cw_ref, par_ref, o_ref, s_ref, ext_ref,
                *, bb, chunk, layered_state, single_chunk):
    c_idx = pl.program_id(1)
    C = chunk
    first_row = SUBLANES - (CONV_WIDTH - 1)
    s_init = s0_ref.at[0] if layered_state else s0_ref
    s_prev = s_init if single_chunk else s_ref

    @pl.when(c_idx == 0)
    def _():
        ext_ref[:, 0:SUBLANES, :] = conv0_ref[...]
        if not single_chunk:
            s_ref[...] = s_init[...]

    @pl.when(c_idx > 0)
    def _():
        ext_ref[:, 0:SUBLANES, :] = ext_ref[:, C:C + SUBLANES, :]

    ext_ref[:, SUBLANES:SUBLANES + C, 0:WIDTH_B] = q_ref[...]
    ext_ref[:, SUBLANES:SUBLANES + C, WIDTH_B:2 * WIDTH_B] = k_ref[...]
    ext_ref[:, SUBLANES:SUBLANES + C, 2 * WIDTH_B:3 * WIDTH_B] = v_ref[...]

    ri = lax.broadcasted_iota(jnp.int32, (C, C), 0)
    ci = lax.broadcasted_iota(jnp.int32, (C, C), 1)
    causal = ri >= ci
    strict = ri > ci
    tril = causal.astype(F32)
    neg_a = -jnp.exp(par_ref[0:1, :])
    dt_bias = par_ref[1:2, :]
    norm_w = par_ref[2:3, :]
    pad_rows = jnp.zeros((HEAD_DIM_B - C, HEAD_DIM_B), F32) if C < HEAD_DIM_B else None

    def pad128(x):
        return x if pad_rows is None else jnp.concatenate([x, pad_rows], axis=0)

    def conv(i, col):
        acc = ext_ref[i, pl.ds(first_row, C), pl.ds(col, HEAD_DIM_B)] * cw_ref[0:1, pl.ds(col, HEAD_DIM_B)]
        for j in range(1, CONV_WIDTH):
            acc = acc + (ext_ref[i, pl.ds(first_row + j, C), pl.ds(col, HEAD_DIM_B)]
                         * cw_ref[j:j + 1, pl.ds(col, HEAD_DIM_B)])
        return _silu(acc)

    def l2n(x):
        return x * lax.rsqrt(jnp.sum(x * x, axis=-1, keepdims=True) + RMS_EPS)

    def body(i, carry):
        ab = ab_ref[i]
        z = ab + dt_bias
        softplus = jnp.maximum(z, 0.0) + jnp.log(1.0 + jnp.exp(-jnp.abs(z)))
        g_all = neg_a * softplus
        beta_all = _sigmoid(ab)
        gcum_all = jnp.dot(tril, g_all, preferred_element_type=F32, precision=HIGHEST)
        gcum_rows = pad128(gcum_all).T
        heads = range(N_HEADS_B)
        states = [s_prev[i, h] for h in heads]
        q = [l2n(conv(i, h * HEAD_DIM_B)) * (HEAD_DIM_B ** -0.5) for h in heads]
        k = [l2n(conv(i, WIDTH_B + h * HEAD_DIM_B)) for h in heads]
        v = [conv(i, 2 * WIDTH_B + h * HEAD_DIM_B) for h in heads]
        beta = [beta_all[:, N_HEADS_B + h:N_HEADS_B + h + 1] for h in heads]
        gcum = [gcum_all[:, h:h + 1] for h in heads]
        decay = [jnp.where(causal, jnp.exp(jnp.minimum(gcum[h] - gcum_rows[h:h + 1, 0:C], 0.0)), 0.0) for h in heads]
        qk = [_bdot_nt(jnp.concatenate([q[h], k[h]], axis=0), k[h]) for h in heads]
        intra = [qk[h][0:C] * decay[h] for h in heads]
        lower = [jnp.where(strict, qk[h][C:2 * C] * beta[h] * decay[h], 0.0) for h in heads]
        inv = _unit_lower_inverses(lower, ri, ci)
        e_g = [jnp.exp(gcum[h]) for h in heads]
        uw = [_bdot(inv[h], jnp.concatenate([v[h] * beta[h], k[h] * beta[h] * e_g[h]], axis=-1)) for h in heads]
        ws_qs = [_bdot(jnp.concatenate([uw[h][:, HEAD_DIM_B:2 * HEAD_DIM_B], q[h] * e_g[h]], axis=0), states[h])
                 for h in heads]
        v_new = [uw[h][:, 0:HEAD_DIM_B] - ws_qs[h][0:C] for h in heads]
        g_last = [gcum[h][C - 1:C, :] for h in heads]
        k_dec = [k[h] * jnp.exp(g_last[h] - gcum[h]) for h in heads]
        o = [ws_qs[h][C:2 * C] + _bdot(intra[h], v_new[h]) for h in heads]
        new_states = [states[h] * jnp.exp(g_last[h]) + _bdot(pad128(k_dec[h]).T, pad128(v_new[h])) for h in heads]
        for h in heads:
            s_ref[i, h] = new_states[h]
            o_h = o[h] * lax.rsqrt(jnp.mean(o[h] * o[h], axis=-1, keepdims=True) + RMS_EPS) * norm_w
            o_ref[i, :, pl.ds(h * HEAD_DIM_B, HEAD_DIM_B)] = o_h
        return carry

    lax.fori_loop(0, bb, body, 0)


def _gated_deltanet(rest, conv0, s0, layer, conv_w, params, bb, chunk):
    nseq, t, _ = rest.shape
    qb = REST_QKVB // WIDTH_B
    col_spec = lambda col: pl.BlockSpec((bb, chunk, WIDTH_B), lambda i, c: (i, c, col))
    state_shape = (bb, N_HEADS_B, HEAD_DIM_B, HEAD_DIM_B)
    state_spec = pl.BlockSpec(state_shape, lambda i, c: (i, 0, 0, 0))
    layered = s0.ndim == 5
    s0_spec = pl.BlockSpec((1,) + state_shape, lambda i, c: (layer, i, 0, 0, 0)) if layered else state_spec
    return pl.pallas_call(
        functools.partial(_gdn_kernel, bb=bb, chunk=chunk, layered_state=layered, single_chunk=(t == chunk)),
        out_shape=[jax.ShapeDtypeStruct((nseq, t, WIDTH_B), F32),
                   jax.ShapeDtypeStruct((nseq, N_HEADS_B, HEAD_DIM_B, HEAD_DIM_B), F32)],
        grid=(nseq // bb, t // chunk),
        in_specs=[col_spec(qb), col_spec(qb + 1), col_spec(qb + 2),
                  pl.BlockSpec((bb, chunk, LANES), lambda i, c: (i, c, REST_AB // LANES)),
                  pl.BlockSpec((bb, SUBLANES, CONV_DIM), lambda i, c: (i, 0, 0)),
                  s0_spec,
                  pl.BlockSpec((SUBLANES, CONV_DIM), lambda i, c: (0, 0)),
                  pl.BlockSpec((SUBLANES, LANES), lambda i, c: (0, 0))],
        out_specs=[pl.BlockSpec((bb, chunk, WIDTH_B), lambda i, c: (i, c, 0)), state_spec],
        scratch_shapes=[pltpu.VMEM((bb, chunk + SUBLANES, CONV_DIM), F32)],
        compiler_params=_cparams(("arbitrary", "arbitrary")),
        name="gated_deltanet",
    )(rest, rest, rest, rest, conv0, s0, conv_w, params)


def _out_kernel(x_ref, mod_ref, o1_ref, o2_ref, o3_ref, l1_ref, l2_ref, l3_ref, za_ref, ob_ref, zb_ref, ga_ref, gb_ref,
                wa_ref, wb_ref, wo_ref, lng_ref, lnb_ref, y_ref):
    bs, bt, _ = x_ref.shape
    rows = bs * bt
    flat = lambda ref: ref[...].reshape(rows, ref.shape[-1])
    group = lambda ref: jnp.concatenate([ref[:, half].reshape(rows, LANES) for half in range(HALVES)], axis=-1)
    l1, l2, l3 = group(l1_ref), group(l2_ref), group(l3_ref)
    m = jnp.maximum(jnp.maximum(l1, l2), l3)
    e1, e2, e3 = jnp.exp(l1 - m), jnp.exp(l2 - m), jnp.exp(l3 - m)
    tot = e1 + e2 + e3
    att = jnp.concatenate([group(o1_ref) * (e1 / tot), group(o2_ref) * (e2 / tot), group(o3_ref) * (e3 / tot)], axis=-1)
    branch_a = jnp.dot((att * _silu(flat(za_ref))).astype(BF16), wa_ref[...], preferred_element_type=F32)
    branch_b = jnp.dot((flat(ob_ref) * _silu(flat(zb_ref))).astype(BF16), wb_ref[...], preferred_element_type=F32)
    merged = _sigmoid(flat(ga_ref)) * branch_a + _sigmoid(flat(gb_ref)) * branch_b
    out = jnp.dot(merged.astype(BF16), wo_ref[...], preferred_element_type=F32).reshape(bs, bt, D_MODEL)
    gate = mod_ref[:, :, 2 * D_MODEL:3 * D_MODEL]
    y = DEEPNORM_ALPHA * x_ref[...] + gate * out
    mu = jnp.mean(y, axis=-1, keepdims=True)
    yc = y - mu
    var = jnp.mean(yc * yc, axis=-1, keepdims=True)
    y_ref[...] = yc * lax.rsqrt(var + LN_EPS) * lng_ref[...] + lnb_ref[...]


def _output_block(x, mod, o_g, lse_g, o_b, rest, wa, wb, wo, ln_g, ln_b, bs, bt):
    nseq, t, _ = x.shape
    row = lambda w, col: pl.BlockSpec((bs, bt, w), lambda i, k: (i, k, col))
    halves = pl.BlockSpec((bs, HALVES, bt, LANES), lambda i, k: (i, 0, k, 0))
    const = lambda shape: pl.BlockSpec(shape, lambda i, k: (0,) * len(shape))
    return pl.pallas_call(
        _out_kernel,
        out_shape=jax.ShapeDtypeStruct((nseq, t, D_MODEL), F32),
        grid=(nseq // bs, t // bt),
        in_specs=[row(D_MODEL, 0),
                  pl.BlockSpec((bs, 1, 3 * D_MODEL), lambda i, k: (i, 0, 0)),
                  halves, halves, halves, halves, halves, halves,
                  row(WIDTH_A, REST_ZA // WIDTH_A),
                  row(WIDTH_B, 0),
                  row(WIDTH_B, REST_ZB // WIDTH_B),
                  row(D_MODEL, REST_GA // D_MODEL),
                  row(D_MODEL, REST_GB // D_MODEL),
                  const((WIDTH_A, D_MODEL)), const((WIDTH_B, D_MODEL)), const((D_MODEL, D_MODEL)),
                  const((1, 1, D_MODEL)), const((1, 1, D_MODEL))],
        out_specs=row(D_MODEL, 0),
        compiler_params=_cparams(("arbitrary", "arbitrary")),
        name="output_block",
    )(x, mod, o_g[0], o_g[1], o_g[2], lse_g[0], lse_g[1], lse_g[2], rest, o_b, rest, rest, rest,
      wa, wb, wo, ln_g.reshape(1, 1, D_MODEL), ln_b.reshape(1, 1, D_MODEL))


def _alibi_slopes():
    i = jnp.arange(1, N_HEADS_A + 1, dtype=F32)
    return jnp.exp2(-8.0 * i / N_HEADS_A).reshape(N_GROUPS, HEADS_PER_GROUP)


def _split_w_in(w_in):
    att = w_in[:, :, :ATT_W]
    pad = jnp.zeros(w_in.shape[:2] + (REST_GA - REST_AB - 2 * N_HEADS_B,), w_in.dtype)
    rest = jnp.concatenate([w_in[:, :, SRC_ZA:SRC_AB], w_in[:, :, SRC_AB:SRC_GA], pad, w_in[:, :, SRC_GA:]], axis=-1)
    return att.astype(BF16), rest.astype(BF16)


def _split_kv(att, group, start):
    b = att.shape[0]
    k = att[:, (N_GROUPS + group) * HALVES:(N_GROUPS + group + 1) * HALVES, start:, :]
    v = att[:, (2 * N_GROUPS + group) * HALVES:(2 * N_GROUPS + group + 1) * HALVES, start:, :]
    kv = jnp.stack([k, v], axis=1)
    kv = kv.transpose(0, 3, 1, 2, 4)
    return kv.reshape(b, kv.shape[1], 2, HEADS_PER_GROUP, HEAD_DIM_A)


def _feature_major(cache):
    d, b, l = cache.shape[:3]
    return cache.transpose(0, 1, 3, 4, 5, 2).reshape(d, b, 2, GROUP_W, l)


def kernel(x_prompt, x_sample, c_prompt, c_sample, cache_win1, cache_win2, cache_win3, state_gdn, state_conv,
           w_ada, b_ada, w_in, conv_w, a_log, dt_bias, gdn_norm, w_out_a, w_out_b, w_o, ln_g, ln_b):
    n_p, seq, _ = x_prompt.shape
    n_s, t_new, _ = x_sample.shape
    caches = tuple(_feature_major(c) for c in (cache_win1, cache_win2, cache_win3))
    sample_attn_bb = (8, 8, 2)
    slopes = _alibi_slopes()

    w_att, w_rest = _split_w_in(w_in)
    wa = w_out_a.astype(BF16)
    wb = w_out_b.astype(BF16)
    wo = w_o.astype(BF16)
    conv_w_p = jnp.pad(conv_w, ((0, 0), (0, SUBLANES - CONV_WIDTH), (0, 0)))
    params = jnp.zeros((DEPTH, SUBLANES, LANES), F32)
    params = params.at[:, 0, :N_HEADS_B].set(a_log).at[:, 1, :N_HEADS_B].set(dt_bias).at[:, 2, :].set(gdn_norm)

    n_all = n_p + n_s
    n_pad = -(-n_all // SUBLANES) * SUBLANES
    c_all = jnp.pad(jnp.concatenate([c_prompt, c_sample], axis=0), ((0, n_pad - n_all), (0, 0)))
    mod_all = _modulation(c_all, w_ada, b_ada)

    zero_conv = jnp.zeros((n_p, SUBLANES, CONV_DIM), F32)
    zero_state = jnp.zeros((n_p, N_HEADS_B, HEAD_DIM_B, HEAD_DIM_B), F32)
    conv_s = jnp.pad(state_conv, ((0, 0), (0, 0), (SUBLANES - (CONV_WIDTH - 1), 0), (0, 0)))

    xp, xs = x_prompt, x_sample
    acc_p = [[] for _ in range(N_GROUPS + 2)]
    acc_s = [[] for _ in range(N_GROUPS + 2)]
    for l in range(DEPTH):
        mod_p = mod_all[l, :n_p, None, :]
        mod_s = mod_all[l, n_p:n_all, None, :]

        att_p, rest_p = _in_projection(xp, mod_p, w_att[l], w_rest[l], 1, 512)
        att_s, rest_s = _in_projection(xs, mod_s, w_att[l], w_rest[l], 32, t_new)

        o_p, lse_p, o_s, lse_s = [], [], [], []
        for g in range(N_GROUPS):
            o, lse = _attention_prompt(att_p, slopes, g)
            o_p.append(o)
            lse_p.append(lse)
            o, lse = _attention_sample(att_s, caches[g], l, slopes, g, sample_attn_bb[g])
            o_s.append(o)
            lse_s.append(lse)

        ob_p, st_p = _gated_deltanet(rest_p, zero_conv, zero_state, 0, conv_w_p[l], params[l], 1, PROMPT_GDN_CHUNK)
        ob_s, st_s = _gated_deltanet(rest_s, conv_s[l], state_gdn, l, conv_w_p[l], params[l], 8, t_new)

        xp_new = _output_block(xp, mod_p, o_p, lse_p, ob_p, rest_p, wa[l], wb[l], wo[l], ln_g[l], ln_b[l], 1, 256)
        xs_new = _output_block(xs, mod_s, o_s, lse_s, ob_s, rest_s, wa[l], wb[l], wo[l], ln_g[l], ln_b[l], 32, t_new)

        for g in range(N_GROUPS):
            acc_p[g].append(_split_kv(att_p, g, seq - min(WINDOWS[g], seq)))
            acc_s[g].append(_split_kv(att_s, g, 0))
        acc_p[N_GROUPS].append(st_p)
        acc_s[N_GROUPS].append(st_s)
        acc_p[N_GROUPS + 1].append(rest_p[:, seq - (CONV_WIDTH - 1):, REST_QKVB:REST_QKVB + CONV_DIM])
        acc_s[N_GROUPS + 1].append(rest_s[:, t_new - (CONV_WIDTH - 1):, REST_QKVB:REST_QKVB + CONV_DIM])
        xp, xs = xp_new, xs_new

    outs_p = [jnp.stack(a) for a in acc_p]
    outs_s = [jnp.stack(a) for a in acc_s]
    return (xp, xs, *outs_p, *outs_s)
```

```python
import functools

import numpy as np
import jax
import jax.numpy as jnp
from jax import lax
from jax.experimental import pallas as pl
from jax.experimental.pallas import tpu as pltpu

F32 = jnp.float32
BF16 = jnp.bfloat16
HIGHEST = lax.Precision.HIGHEST

D_MODEL = 1024
DEPTH = 4
DILATIONS = (1, 4, 16)
WINDOWS = (128, 512, 2048)
N_GROUPS = 3
HEADS_PER_GROUP = 4
HEAD_DIM_A = 64
GROUP_W = HEADS_PER_GROUP * HEAD_DIM_A
WIDTH_A = N_GROUPS * GROUP_W
N_HEADS_A = N_GROUPS * HEADS_PER_GROUP
STEPS_BACK = 128
A_BLOCK = 128
N_HEADS_B = 6
HEAD_DIM_B = 128
WIDTH_B = N_HEADS_B * HEAD_DIM_B
CONV_WIDTH = 4
CONV_DIM = 3 * WIDTH_B
DEEPNORM_ALPHA = (2 * DEPTH) ** 0.25
LN_EPS = 1e-5
RMS_EPS = 1e-6
LANES = 128
SUBLANES = 8
HALVES = GROUP_W // LANES

SRC_ZA = 3 * WIDTH_A
SRC_QKVB = SRC_ZA + WIDTH_A
SRC_ZB = SRC_QKVB + CONV_DIM
SRC_AB = SRC_ZB + WIDTH_B
SRC_GA = SRC_AB + 2 * N_HEADS_B
SRC_GB = SRC_GA + D_MODEL

ATT_W = 3 * WIDTH_A
ATT_BLOCKS = ATT_W // LANES
REST_ZA = 0
REST_QKVB = 768
REST_ZB = 3072
REST_AB = 3840
REST_GA = 4096
REST_GB = 5120
REST_W = 6144
REST_TILE = 2048

PROMPT_GDN_CHUNK = 128
PROMPT_GDN_SEQS = 2
TRI_BASE = 16
ATTN_TASKS = 2
ATTN_STATIC_BLOCKS = 4
VMEM_LIMIT = 48 * 1024 * 1024

NT_DIMS = (((1,), (1,)), ((), ()))


def _cparams(sem):
    return pltpu.CompilerParams(dimension_semantics=sem, vmem_limit_bytes=VMEM_LIMIT)


def _sigmoid(x):
    return 0.5 * jnp.tanh(0.5 * x) + 0.5


def _silu(x):
    return x * _sigmoid(x)


def _bdot(a, b):
    return jnp.dot(a.astype(BF16), b.astype(BF16), preferred_element_type=F32)


def _bdot_nt(a, b):
    return lax.dot_general(a.astype(BF16), b.astype(BF16), NT_DIMS, preferred_element_type=F32)


def _mod_kernel(c_ref, w_ref, b_ref, o_ref):
    o_ref[0] = _bdot(_silu(c_ref[...]), w_ref[0]) + b_ref[0]


def _modulation(c_all, w_ada, b_ada):
    n = c_all.shape[0]
    return pl.pallas_call(
        _mod_kernel,
        out_shape=jax.ShapeDtypeStruct((DEPTH, n, 3 * D_MODEL), F32),
        grid=(DEPTH, 3),
        in_specs=[
            pl.BlockSpec((n, D_MODEL), lambda l, j: (0, 0)),
            pl.BlockSpec((1, D_MODEL, D_MODEL), lambda l, j: (l, 0, j)),
            pl.BlockSpec((1, 1, D_MODEL), lambda l, j: (l, 0, j)),
        ],
        out_specs=pl.BlockSpec((1, n, D_MODEL), lambda l, j: (l, 0, j)),
        compiler_params=_cparams(("arbitrary", "arbitrary")),
        name="adaln_modulation",
    )(c_all, w_ada, b_ada.reshape(DEPTH, 1, 3 * D_MODEL))


def _modulated(x_ref, mod_ref):
    bs, bt, _ = x_ref.shape
    shift = mod_ref[:, :, 0:D_MODEL]
    scale = mod_ref[:, :, D_MODEL:2 * D_MODEL]
    h = x_ref[...] * (1.0 + scale) + shift
    return h.reshape(bs * bt, D_MODEL).astype(BF16)


def _inproj_att_kernel(x_ref, mod_ref, w_ref, o_ref):
    bs, bt, _ = x_ref.shape
    res = jnp.dot(_modulated(x_ref, mod_ref), w_ref[0], preferred_element_type=F32)
    for j in range(ATT_BLOCKS):
        o_ref[:, j] = res[:, j * LANES:(j + 1) * LANES].reshape(bs, bt, LANES)


def _inproj_rest_kernel(x_ref, mod_ref, w_ref, o_ref):
    bs, bt, _ = x_ref.shape
    res = jnp.dot(_modulated(x_ref, mod_ref), w_ref[0], preferred_element_type=F32)
    o_ref[...] = res.reshape(bs, bt, REST_TILE)


def _in_projection(x, mod, w_att, w_rest, layer, bs, bt):
    nseq, t, _ = x.shape
    att = pl.pallas_call(
        _inproj_att_kernel,
        out_shape=jax.ShapeDtypeStruct((nseq, ATT_BLOCKS, t, LANES), F32),
        grid=(nseq // bs, t // bt),
        in_specs=[
            pl.BlockSpec((bs, bt, D_MODEL), lambda i, k: (i, k, 0)),
            pl.BlockSpec((bs, 1, 3 * D_MODEL), lambda i, k: (i, 0, 0)),
            pl.BlockSpec((1, D_MODEL, ATT_W), lambda i, k: (layer, 0, 0)),
        ],
        out_specs=pl.BlockSpec((bs, ATT_BLOCKS, bt, LANES), lambda i, k: (i, 0, k, 0)),
        compiler_params=_cparams(("arbitrary", "arbitrary")),
        name="in_projection_att",
    )(x, mod, w_att)
    rest = pl.pallas_call(
        _inproj_rest_kernel,
        out_shape=jax.ShapeDtypeStruct((nseq, t, REST_W), F32),
        grid=(REST_W // REST_TILE, nseq // bs, t // bt),
        in_specs=[
            pl.BlockSpec((bs, bt, D_MODEL), lambda j, i, k: (i, k, 0)),
            pl.BlockSpec((bs, 1, 3 * D_MODEL), lambda j, i, k: (i, 0, 0)),
            pl.BlockSpec((1, D_MODEL, REST_TILE), lambda j, i, k: (layer, 0, j)),
        ],
        out_specs=pl.BlockSpec((bs, bt, REST_TILE), lambda j, i, k: (i, k, j)),
        compiler_params=_cparams(("arbitrary", "arbitrary", "arbitrary")),
        name="in_projection_rest",
    )(x, mod, w_rest)
    return att, rest


def _attn_prompt_kernel(slopes_ref, q_ref, kp_ref, ko_ref, vp_ref, vo_ref, o_ref, lse_ref, *, group, dilation, units):
    n = pl.program_id(1)
    span = A_BLOCK * dilation
    qi = lax.broadcasted_iota(jnp.int32, (A_BLOCK, 2 * A_BLOCK), 0)
    kj = lax.broadcasted_iota(jnp.int32, (A_BLOCK, 2 * A_BLOCK), 1)
    steps = A_BLOCK + qi - kj
    band = (steps >= 0) & (steps <= STEPS_BACK)
    first_span = band & ((kj >= A_BLOCK) | (n > 0))
    dist = (steps * dilation).astype(F32)
    lane_head = lax.broadcasted_iota(jnp.int32, (1, GROUP_W), 1) // HEAD_DIM_A
    heads = range(HEADS_PER_GROUP)

    def rows_of(ref, rows):
        return jnp.concatenate([ref[0, half, rows, :] for half in range(HALVES)], axis=-1)

    def run(tasks):
        q, k, v, valid, own_rows = [], [], [], [], []
        for u, r in tasks:
            own = pl.ds(u * span + r, A_BLOCK, stride=dilation)
            if u == 0:
                prev_k, prev_v, prev = kp_ref, vp_ref, pl.ds(r, A_BLOCK, stride=dilation)
            else:
                prev_k, prev_v, prev = ko_ref, vo_ref, pl.ds((u - 1) * span + r, A_BLOCK, stride=dilation)
            own_rows.append(own)
            valid.append(first_span if u == 0 else band)
            q.append(rows_of(q_ref, own) * (HEAD_DIM_A ** -0.5))
            k.append(jnp.concatenate([rows_of(prev_k, prev), rows_of(ko_ref, own)], axis=0).astype(BF16))
            v.append(jnp.concatenate([rows_of(prev_v, prev), rows_of(vo_ref, own)], axis=0).astype(BF16))
        chains = [(t, h) for t in range(len(tasks)) for h in heads]
        s = [lax.dot_general(jnp.where(lane_head == h, q[t], 0.0).astype(BF16), k[t], NT_DIMS,
                             preferred_element_type=F32) for t, h in chains]
        s = [jnp.where(valid[t], s[c] - slopes_ref[group, h] * dist, -jnp.inf) for c, (t, h) in enumerate(chains)]
        m = [x.max(-1, keepdims=True) for x in s]
        p = [jnp.exp(x - mx) for x, mx in zip(s, m)]
        den = [x.sum(-1, keepdims=True) for x in p]
        oh = [jnp.dot((p[c] / den[c]).astype(BF16), v[t], preferred_element_type=F32) for c, (t, h) in enumerate(chains)]
        for t in range(len(tasks)):
            o_acc = jnp.zeros((A_BLOCK, GROUP_W), F32)
            lse_acc = jnp.zeros((A_BLOCK, GROUP_W), F32)
            for h in heads:
                c = t * HEADS_PER_GROUP + h
                o_acc = jnp.where(lane_head == h, oh[c], o_acc)
                lse_acc = jnp.where(lane_head == h, m[c] + jnp.log(den[c]), lse_acc)
            for half in range(HALVES):
                o_ref[0, half, own_rows[t], :] = o_acc[:, half * LANES:(half + 1) * LANES]
                lse_ref[0, half, own_rows[t], :] = lse_acc[:, half * LANES:(half + 1) * LANES]

    if dilation * units <= ATTN_STATIC_BLOCKS:
        blocks = [(u, r) for u in range(units) for r in range(dilation)]
        for j in range(0, len(blocks), ATTN_TASKS):
            run(blocks[j:j + ATTN_TASKS])
    else:
        for u in range(units):
            def residues(j, carry, u=u):
                run([(u, j * ATTN_TASKS + t) for t in range(ATTN_TASKS)])
                return carry
            lax.fori_loop(0, dilation // ATTN_TASKS, residues, 0)


def _attention_prompt(att, slopes, group):
    b, _, s, _ = att.shape
    d = DILATIONS[group]
    span = A_BLOCK * d
    units = max(1, 512 // span)
    rows = span * units
    qb, kb, vb = group, N_GROUPS + group, 2 * N_GROUPS + group
    own = lambda col: pl.BlockSpec((1, HALVES, rows, LANES), lambda i, n: (i, col, n, 0))
    prev = lambda col: pl.BlockSpec((1, HALVES, span, LANES), lambda i, n: (i, col, jnp.maximum(n * units - 1, 0), 0))
    out_spec = pl.BlockSpec((1, HALVES, rows, LANES), lambda i, n: (i, 0, n, 0))
    return pl.pallas_call(
        functools.partial(_attn_prompt_kernel, group=group, dilation=d, units=units),
        out_shape=[jax.ShapeDtypeStruct((b, HALVES, s, LANES), F32)] * 2,
        grid=(b, s // rows),
        in_specs=[pl.BlockSpec(memory_space=pltpu.SMEM), own(qb), prev(kb), own(kb), prev(vb), own(vb)],
        out_specs=[out_spec, out_spec],
        compiler_params=_cparams(("arbitrary", "arbitrary")),
        name=f"attn_prompt_g{group}",
    )(slopes, att, att, att, att, att)


def _sample_bias(slopes, group, t_new):
    d = DILATIONS[group]
    lb = WINDOWS[group]
    t = np.arange(t_new)

    def table(key_pos, width):
        diff = (lb + t)[:, None] - key_pos[None, :]
        ok = (diff >= 0) & (diff % d == 0) & (diff // d <= STEPS_BACK)
        dist = jnp.asarray(np.where(ok, diff, 0), F32)
        bias = jnp.where(jnp.asarray(ok)[None], -slopes[group][:, None, None] * dist[None], -jnp.inf)
        bias = bias.reshape(HEADS_PER_GROUP * t_new, key_pos.shape[0])
        return jnp.pad(bias, ((0, 0), (0, width - key_pos.shape[0])), constant_values=-jnp.inf)

    return table(np.arange(lb), lb), table(lb + t, LANES)


def _attn_sample_kernel(q_ref, k_ref, v_ref, cache_ref, bias_ref, biasn_ref, o_ref, lse_ref, *, bb, per_iter, t_new):
    nq = HEADS_PER_GROUP * t_new
    row_head = lax.broadcasted_iota(jnp.int32, (nq, GROUP_W), 0) // t_new
    lane_head2 = lax.broadcasted_iota(jnp.int32, (nq, GROUP_W), 1) // HEAD_DIM_A
    lane_head = lax.broadcasted_iota(jnp.int32, (1, GROUP_W), 1) // HEAD_DIM_A
    pad_n = jnp.zeros((LANES - t_new, GROUP_W), F32)

    def new_rows(ref, i):
        return jnp.concatenate([ref[i, half] for half in range(HALVES)], axis=-1)

    def body(it, carry):
        seqs = [it * per_iter + s for s in range(per_iter)]
        each = lambda f: [f(i) for i in seqs]
        n = range(per_iter)
        q = each(lambda i: new_rows(q_ref, i) * (HEAD_DIM_A ** -0.5))
        qm = [jnp.where(row_head == lane_head2, jnp.concatenate([x] * HEADS_PER_GROUP, axis=0), 0.0).astype(BF16)
              for x in q]
        kt = each(lambda i: cache_ref[0, i, 0].astype(BF16))
        vt = each(lambda i: cache_ref[0, i, 1].astype(BF16))
        kn = each(lambda i: jnp.concatenate([new_rows(k_ref, i), pad_n], axis=0).astype(BF16))
        vn = each(lambda i: jnp.concatenate([new_rows(v_ref, i), pad_n], axis=0).astype(BF16))
        sc = [jnp.dot(qm[c], kt[c], preferred_element_type=F32) + bias_ref[...] for c in n]
        sn = [lax.dot_general(qm[c], kn[c], NT_DIMS, preferred_element_type=F32) + biasn_ref[...] for c in n]
        m = [jnp.maximum(sc[c].max(-1, keepdims=True), sn[c].max(-1, keepdims=True)) for c in n]
        pc = [jnp.exp(sc[c] - m[c]) for c in n]
        pn = [jnp.exp(sn[c] - m[c]) for c in n]
        den = [pc[c].sum(-1, keepdims=True) + pn[c].sum(-1, keepdims=True) for c in n]
        of = [lax.dot_general((pc[c] / den[c]).astype(BF16), vt[c], NT_DIMS, preferred_element_type=F32)
              + jnp.dot((pn[c] / den[c]).astype(BF16), vn[c], preferred_element_type=F32) for c in n]
        for c, i in enumerate(seqs):
            lse_col = m[c] + jnp.log(den[c])
            o = jnp.zeros((t_new, GROUP_W), F32)
            lse = jnp.zeros((t_new, GROUP_W), F32)
            for h in range(HEADS_PER_GROUP):
                head = lane_head == h
                o = jnp.where(head, of[c][h * t_new:(h + 1) * t_new, :], o)
                lse = jnp.where(head, lse_col[h * t_new:(h + 1) * t_new, :], lse)
            for half in range(HALVES):
                o_ref[i, half] = o[:, half * LANES:(half + 1) * LANES]
                lse_ref[i, half] = lse[:, half * LANES:(half + 1) * LANES]
        return carry

    lax.fori_loop(0, bb // per_iter, body, 0)


def _attention_sample(att, cache_t, layer, slopes, group, bb, per_iter):
    nseq, _, t_new, _ = att.shape
    lb = WINDOWS[group]
    bias, bias_new = _sample_bias(slopes, group, t_new)
    qb, kb, vb = group, N_GROUPS + group, 2 * N_GROUPS + group
    new_spec = lambda col: pl.BlockSpec((bb, HALVES, t_new, LANES), lambda i: (i, col, 0, 0))
    out_spec = pl.BlockSpec((bb, HALVES, t_new, LANES), lambda i: (i, 0, 0, 0))
    return pl.pallas_call(
        functools.partial(_attn_sample_kernel, bb=bb, per_iter=per_iter, t_new=t_new),
        out_shape=[jax.ShapeDtypeStruct((nseq, HALVES, t_new, LANES), F32)] * 2,
        grid=(nseq // bb,),
        in_specs=[new_spec(qb), new_spec(kb), new_spec(vb),
                  pl.BlockSpec((1, bb, 2, GROUP_W, lb), lambda i: (layer, i, 0, 0, 0)),
                  pl.BlockSpec(bias.shape, lambda i: (0, 0)), pl.BlockSpec(bias_new.shape, lambda i: (0, 0))],
        out_specs=[out_spec, out_spec],
        compiler_params=_cparams(("arbitrary",)),
        name=f"attn_sample_g{group}",
    )(att, att, att, cache_t, bias, bias_new)


def _unit_lower_inverses(lowers, ri, ci):
    c = lowers[0].shape[0]
    base = min(TRI_BASE, c)
    ident = (ri == ci).astype(F32)
    same = (ri // base) == (ci // base)
    powers = [jnp.where(same, l, 0.0) for l in lowers]
    invs = [ident - p for p in powers]
    for _ in range(int(np.log2(base)) - 1):
        powers = [_bdot(p, p) for p in powers]
        invs = [t + _bdot(t, p) for t, p in zip(invs, powers)]
    size = base
    while size < c:
        pair = ((ri // (2 * size)) == (ci // (2 * size))) & ((ri // size) % 2 == 1) & ((ci // size) % 2 == 0)
        offs = [_bdot(jnp.where(pair, l, 0.0), t) for l, t in zip(lowers, invs)]
        invs = [t - _bdot(t, o) for t, o in zip(invs, offs)]
        size *= 2
    return invs


def _gdn_kernel(*refs, bb, per_iter, chunk, layered, single_chunk, aliased):
    (q_ref, k_ref, v_ref, ab_ref, conv0_ref, s0_ref, cw_ref, par_ref), refs = refs[:8], refs[8 + int(aliased):]
    o_ref, s_ref, ext_ref = refs
    c_idx = pl.program_id(1)
    C = chunk
    first_row = SUBLANES - (CONV_WIDTH - 1)
    if layered:
        s0_ref, s_ref = s0_ref.at[0], s_ref.at[0]
    cw_ref, par_ref = cw_ref.at[0], par_ref.at[0]
    s_prev = s0_ref if single_chunk else s_ref

    @pl.when(c_idx == 0)
    def _():
        ext_ref[:, 0:SUBLANES, :] = conv0_ref[...]
        if not single_chunk:
            s_ref[...] = s0_ref[...]

    @pl.when(c_idx > 0)
    def _():
        ext_ref[:, 0:SUBLANES, :] = ext_ref[:, C:C + SUBLANES, :]

    ext_ref[:, SUBLANES:SUBLANES + C, 0:WIDTH_B] = q_ref[...]
    ext_ref[:, SUBLANES:SUBLANES + C, WIDTH_B:2 * WIDTH_B] = k_ref[...]
    ext_ref[:, SUBLANES:SUBLANES + C, 2 * WIDTH_B:3 * WIDTH_B] = v_ref[...]

    ri = lax.broadcasted_iota(jnp.int32, (C, C), 0)
    ci = lax.broadcasted_iota(jnp.int32, (C, C), 1)
    causal = ri >= ci
    strict = ri > ci
    tril = causal.astype(F32)
    neg_a = -jnp.exp(par_ref[0:1, :])
    dt_bias = par_ref[1:2, :]
    norm_w = par_ref[2:3, :]
    pad_rows = jnp.zeros((HEAD_DIM_B - C, HEAD_DIM_B), F32) if C < HEAD_DIM_B else None

    def pad128(x):
        return x if pad_rows is None else jnp.concatenate([x, pad_rows], axis=0)

    def conv(i, col):
        acc = ext_ref[i, pl.ds(first_row, C), pl.ds(col, HEAD_DIM_B)] * cw_ref[0:1, pl.ds(col, HEAD_DIM_B)]
        for j in range(1, CONV_WIDTH):
            acc = acc + (ext_ref[i, pl.ds(first_row + j, C), pl.ds(col, HEAD_DIM_B)]
                         * cw_ref[j:j + 1, pl.ds(col, HEAD_DIM_B)])
        return _silu(acc)

    def l2n(x):
        return x * lax.rsqrt(jnp.sum(x * x, axis=-1, keepdims=True) + RMS_EPS)

    def gates(i):
        ab = ab_ref[i]
        z = ab + dt_bias
        softplus = jnp.maximum(z, 0.0) + jnp.log(1.0 + jnp.exp(-jnp.abs(z)))
        g_all = neg_a * softplus
        beta_all = _sigmoid(ab)
        gcum_all = jnp.dot(tril, g_all, preferred_element_type=F32, precision=HIGHEST)
        return beta_all, gcum_all, pad128(gcum_all).T

    def body(it, carry):
        seqs = [it * per_iter + s for s in range(per_iter)]
        beta_all, gcum_all, gcum_rows = zip(*[gates(i) for i in seqs])
        chains = [(s, h) for s in range(per_iter) for h in range(N_HEADS_B)]
        each = lambda f: [f(s, h) for s, h in chains]
        n = range(len(chains))
        states = each(lambda s, h: s_prev[seqs[s], h])
        q = each(lambda s, h: l2n(conv(seqs[s], h * HEAD_DIM_B)) * (HEAD_DIM_B ** -0.5))
        k = each(lambda s, h: l2n(conv(seqs[s], WIDTH_B + h * HEAD_DIM_B)))
        v = each(lambda s, h: conv(seqs[s], 2 * WIDTH_B + h * HEAD_DIM_B))
        beta = each(lambda s, h: beta_all[s][:, N_HEADS_B + h:N_HEADS_B + h + 1])
        gcum = each(lambda s, h: gcum_all[s][:, h:h + 1])
        grow = each(lambda s, h: gcum_rows[s][h:h + 1, 0:C])
        decay = [jnp.where(causal, jnp.exp(jnp.minimum(gcum[c] - grow[c], 0.0)), 0.0) for c in n]
        qk = [_bdot_nt(jnp.concatenate([q[c], k[c]], axis=0), k[c]) for c in n]
        intra = [qk[c][0:C] * decay[c] for c in n]
        lower = [jnp.where(strict, qk[c][C:2 * C] * beta[c] * decay[c], 0.0) for c in n]
        inv = _unit_lower_inverses(lower, ri, ci)
        e_g = [jnp.exp(gcum[c]) for c in n]
        uw = [_bdot(inv[c], jnp.concatenate([v[c] * beta[c], k[c] * beta[c] * e_g[c]], axis=-1)) for c in n]
        ws_qs = [_bdot(jnp.concatenate([uw[c][:, HEAD_DIM_B:2 * HEAD_DIM_B], q[c] * e_g[c]], axis=0), states[c])
                 for c in n]
        v_new = [uw[c][:, 0:HEAD_DIM_B] - ws_qs[c][0:C] for c in n]
        g_last = [gcum[c][C - 1:C, :] for c in n]
        k_dec = [k[c] * jnp.exp(g_last[c] - gcum[c]) for c in n]
        o = [ws_qs[c][C:2 * C] + _bdot(intra[c], v_new[c]) for c in n]
        new_states = [states[c] * jnp.exp(g_last[c]) + _bdot(pad128(k_dec[c]).T, pad128(v_new[c])) for c in n]
        for c, (s, h) in enumerate(chains):
            s_ref[seqs[s], h] = new_states[c]
            o_c = o[c] * lax.rsqrt(jnp.mean(o[c] * o[c], axis=-1, keepdims=True) + RMS_EPS) * norm_w
            o_ref[seqs[s], :, pl.ds(h * HEAD_DIM_B, HEAD_DIM_B)] = o_c
        return carry

    lax.fori_loop(0, bb // per_iter, body, 0)


def _gated_deltanet(rest, conv0, s0, layer, conv_w, params, bb, per_iter, chunk, state_buf=None):
    nseq, t, _ = rest.shape
    qb = REST_QKVB // WIDTH_B
    col_spec = lambda col: pl.BlockSpec((bb, chunk, WIDTH_B), lambda i, c: (i, c, col))
    state_shape = (bb, N_HEADS_B, HEAD_DIM_B, HEAD_DIM_B)
    layered = s0.ndim == 5
    if layered:
        state_spec = pl.BlockSpec((1,) + state_shape, lambda i, c: (layer, i, 0, 0, 0))
        state_out = jax.ShapeDtypeStruct(s0.shape, F32)
    else:
        state_spec = pl.BlockSpec(state_shape, lambda i, c: (i, 0, 0, 0))
        state_out = jax.ShapeDtypeStruct((nseq,) + state_shape[1:], F32)
    in_specs = [col_spec(qb), col_spec(qb + 1), col_spec(qb + 2),
                pl.BlockSpec((bb, chunk, LANES), lambda i, c: (i, c, REST_AB // LANES)),
                pl.BlockSpec((bb, SUBLANES, CONV_DIM), lambda i, c: (i, 0, 0)),
                state_spec,
                pl.BlockSpec((1, SUBLANES, CONV_DIM), lambda i, c: (layer, 0, 0)),
                pl.BlockSpec((1, SUBLANES, LANES), lambda i, c: (layer, 0, 0))]
    args = [rest, rest, rest, rest, conv0, s0, conv_w, params]
    aliases = {}
    if state_buf is not None:
        aliases = {len(args): 1}
        in_specs.append(pl.BlockSpec(memory_space=pl.ANY))
        args.append(state_buf)
    return pl.pallas_call(
        functools.partial(_gdn_kernel, bb=bb, per_iter=per_iter, chunk=chunk, layered=layered,
                          single_chunk=(t == chunk), aliased=state_buf is not None),
        out_shape=[jax.ShapeDtypeStruct((nseq, t, WIDTH_B), F32), state_out],
        grid=(nseq // bb, t // chunk),
        in_specs=in_specs,
        out_specs=[pl.BlockSpec((bb, chunk, WIDTH_B), lambda i, c: (i, c, 0)), state_spec],
        scratch_shapes=[pltpu.VMEM((bb, chunk + SUBLANES, CONV_DIM), F32)],
        input_output_aliases=aliases,
        compiler_params=_cparams(("arbitrary", "arbitrary")),
        name="gated_deltanet",
    )(*args)


def _out_kernel(x_ref, mod_ref, o1_ref, o2_ref, o3_ref, l1_ref, l2_ref, l3_ref, za_ref, ob_ref, zb_ref, ga_ref, gb_ref,
                wa_ref, wb_ref, wo_ref, lng_ref, lnb_ref, y_ref):
    bs, bt, _ = x_ref.shape
    rows = bs * bt
    flat = lambda ref: ref[...].reshape(rows, ref.shape[-1])
    group = lambda ref: jnp.concatenate([ref[:, half].reshape(rows, LANES) for half in range(HALVES)], axis=-1)
    l1, l2, l3 = group(l1_ref), group(l2_ref), group(l3_ref)
    m = jnp.maximum(jnp.maximum(l1, l2), l3)
    e1, e2, e3 = jnp.exp(l1 - m), jnp.exp(l2 - m), jnp.exp(l3 - m)
    inv_tot = 1.0 / (e1 + e2 + e3)
    att = jnp.concatenate([group(o1_ref) * (e1 * inv_tot), group(o2_ref) * (e2 * inv_tot),
                           group(o3_ref) * (e3 * inv_tot)], axis=-1)
    branch_a = jnp.dot((att * _silu(flat(za_ref))).astype(BF16), wa_ref[0], preferred_element_type=F32)
    branch_b = jnp.dot((flat(ob_ref) * _silu(flat(zb_ref))).astype(BF16), wb_ref[0], preferred_element_type=F32)
    merged = _sigmoid(flat(ga_ref)) * branch_a + _sigmoid(flat(gb_ref)) * branch_b
    out = jnp.dot(merged.astype(BF16), wo_ref[0], preferred_element_type=F32).reshape(bs, bt, D_MODEL)
    gate = mod_ref[:, :, 2 * D_MODEL:3 * D_MODEL]
    y = DEEPNORM_ALPHA * x_ref[...] + gate * out
    mu = jnp.mean(y, axis=-1, keepdims=True)
    yc = y - mu
    var = jnp.mean(yc * yc, axis=-1, keepdims=True)
    y_ref[...] = yc * lax.rsqrt(var + LN_EPS) * lng_ref[...] + lnb_ref[...]


def _output_block(x, mod, o_g, lse_g, o_b, rest, wa, wb, wo, ln_g, ln_b, layer, bs, bt):
    nseq, t, _ = x.shape
    row = lambda w, col: pl.BlockSpec((bs, bt, w), lambda i, k: (i, k, col))
    halves = pl.BlockSpec((bs, HALVES, bt, LANES), lambda i, k: (i, 0, k, 0))
    per_layer = lambda shape: pl.BlockSpec((1,) + shape, lambda i, k: (layer,) + (0,) * len(shape))
    return pl.pallas_call(
        _out_kernel,
        out_shape=jax.ShapeDtypeStruct((nseq, t, D_MODEL), F32),
        grid=(nseq // bs, t // bt),
        in_specs=[row(D_MODEL, 0),
                  pl.BlockSpec((bs, 1, 3 * D_MODEL), lambda i, k: (i, 0, 0)),
                  halves, halves, halves, halves, halves, halves,
                  row(WIDTH_A, REST_ZA // WIDTH_A),
                  row(WIDTH_B, 0),
                  row(WIDTH_B, REST_ZB // WIDTH_B),
                  row(D_MODEL, REST_GA // D_MODEL),
                  row(D_MODEL, REST_GB // D_MODEL),
                  per_layer((WIDTH_A, D_MODEL)), per_layer((WIDTH_B, D_MODEL)), per_layer((D_MODEL, D_MODEL)),
                  per_layer((1, D_MODEL)), per_layer((1, D_MODEL))],
        out_specs=row(D_MODEL, 0),
        compiler_params=_cparams(("arbitrary", "arbitrary")),
        name="output_block",
    )(x, mod, o_g[0], o_g[1], o_g[2], lse_g[0], lse_g[1], lse_g[2], rest, o_b, rest, rest, rest,
      wa, wb, wo, ln_g, ln_b)


def _alibi_slopes():
    i = jnp.arange(1, N_HEADS_A + 1, dtype=F32)
    return jnp.exp2(-8.0 * i / N_HEADS_A).reshape(N_GROUPS, HEADS_PER_GROUP)


def _split_w_in(w_in):
    pad = jnp.zeros(w_in.shape[:2] + (REST_GA - REST_AB - 2 * N_HEADS_B,), BF16)
    rest = jnp.concatenate([w_in[:, :, SRC_ZA:SRC_GA].astype(BF16), pad, w_in[:, :, SRC_GA:].astype(BF16)], axis=-1)
    return w_in[:, :, :ATT_W].astype(BF16), rest


def _split_kv(att, group, start):
    b = att.shape[0]
    k = att[:, (N_GROUPS + group) * HALVES:(N_GROUPS + group + 1) * HALVES, start:, :]
    v = att[:, (2 * N_GROUPS + group) * HALVES:(2 * N_GROUPS + group + 1) * HALVES, start:, :]
    kv = jnp.stack([k, v], axis=1)
    kv = kv.transpose(0, 3, 1, 2, 4)
    return kv.reshape(b, kv.shape[1], 2, HEADS_PER_GROUP, HEAD_DIM_A)


def _feature_major(cache):
    d, b, l = cache.shape[:3]
    return cache.transpose(0, 1, 3, 4, 5, 2).reshape(d, b, 2, GROUP_W, l)


def kernel(x_prompt, x_sample, c_prompt, c_sample, cache_win1, cache_win2, cache_win3, state_gdn, state_conv,
           w_ada, b_ada, w_in, conv_w, a_log, dt_bias, gdn_norm, w_out_a, w_out_b, w_o, ln_g, ln_b):
    n_p, seq, _ = x_prompt.shape
    n_s, t_new, _ = x_sample.shape
    caches = tuple(_feature_major(c) for c in (cache_win1, cache_win2, cache_win3))
    sample_attn_tiling = ((8, 4), (8, 2), (2, 2))
    slopes = _alibi_slopes()

    w_att, w_rest = _split_w_in(w_in)
    wa = w_out_a.astype(BF16)
    wb = w_out_b.astype(BF16)
    wo = w_o.astype(BF16)
    ln_g3 = ln_g.reshape(DEPTH, 1, D_MODEL)
    ln_b3 = ln_b.reshape(DEPTH, 1, D_MODEL)
    conv_w_p = jnp.pad(conv_w, ((0, 0), (0, SUBLANES - CONV_WIDTH), (0, 0)))
    params = jnp.zeros((DEPTH, SUBLANES, LANES), F32)
    params = params.at[:, 0, :N_HEADS_B].set(a_log).at[:, 1, :N_HEADS_B].set(dt_bias).at[:, 2, :].set(gdn_norm)

    n_all = n_p + n_s
    n_pad = -(-n_all // SUBLANES) * SUBLANES
    c_all = jnp.pad(jnp.concatenate([c_prompt, c_sample], axis=0), ((0, n_pad - n_all), (0, 0)))
    mod_all = _modulation(c_all, w_ada, b_ada)

    zero_conv = jnp.zeros((n_p, SUBLANES, CONV_DIM), F32)
    zero_state = jnp.zeros((n_p, N_HEADS_B, HEAD_DIM_B, HEAD_DIM_B), F32)
    conv_s = jnp.pad(state_conv, ((0, 0), (0, 0), (SUBLANES - (CONV_WIDTH - 1), 0), (0, 0)))

    xp, xs = x_prompt, x_sample
    acc_p = [[] for _ in range(N_GROUPS + 2)]
    acc_s = [[] for _ in range(N_GROUPS + 2)]
    gdn_sample = None
    for l in range(DEPTH):
        mod_p = mod_all[l, :n_p, None, :]
        mod_s = mod_all[l, n_p:n_all, None, :]

        att_p, rest_p = _in_projection(xp, mod_p, w_att, w_rest, l, 1, 512)
        att_s, rest_s = _in_projection(xs, mod_s, w_att, w_rest, l, 32, t_new)

        o_p, lse_p, o_s, lse_s = [], [], [], []
        for g in range(N_GROUPS):
            o, lse = _attention_prompt(att_p, slopes, g)
            o_p.append(o)
            lse_p.append(lse)
            o, lse = _attention_sample(att_s, caches[g], l, slopes, g, *sample_attn_tiling[g])
            o_s.append(o)
            lse_s.append(lse)

        ob_p, st_p = _gated_deltanet(rest_p, zero_conv, zero_state, l, conv_w_p, params, PROMPT_GDN_SEQS,
                                     PROMPT_GDN_SEQS, PROMPT_GDN_CHUNK)
        ob_s, gdn_sample = _gated_deltanet(rest_s, conv_s[l], state_gdn, l, conv_w_p, params, 8, 2, t_new,
                                           state_buf=gdn_sample)

        xp_new = _output_block(xp, mod_p, o_p, lse_p, ob_p, rest_p, wa, wb, wo, ln_g3, ln_b3, l, 1, 256)
        xs_new = _output_block(xs, mod_s, o_s, lse_s, ob_s, rest_s, wa, wb, wo, ln_g3, ln_b3, l, 32, t_new)

        for g in range(N_GROUPS):
            acc_p[g].append(_split_kv(att_p, g, seq - min(WINDOWS[g], seq)))
            acc_s[g].append(_split_kv(att_s, g, 0))
        acc_p[N_GROUPS].append(st_p)
        acc_p[N_GROUPS + 1].append(rest_p[:, seq - (CONV_WIDTH - 1):, REST_QKVB:REST_QKVB + CONV_DIM])
        acc_s[N_GROUPS + 1].append(rest_s[:, t_new - (CONV_WIDTH - 1):, REST_QKVB:REST_QKVB + CONV_DIM])
        xp, xs = xp_new, xs_new

    acc_s[N_GROUPS] = None
    outs_p = [jnp.stack(a) for a in acc_p]
    outs_s = [gdn_sample if a is None else jnp.stack(a) for a in acc_s]
    return (xp, xs, *outs_p, *outs_s)
```

```python
import functools

import numpy as np
import jax
import jax.numpy as jnp
from jax import lax
from jax.experimental import pallas as pl
from jax.experimental.pallas import tpu as pltpu

F32 = jnp.float32
BF16 = jnp.bfloat16
HIGHEST = lax.Precision.HIGHEST

D_MODEL = 1024
DEPTH = 4
DILATIONS = (1, 4, 16)
WINDOWS = (128, 512, 2048)
N_GROUPS = 3
HEADS_PER_GROUP = 4
HEAD_DIM_A = 64
GROUP_W = HEADS_PER_GROUP * HEAD_DIM_A
WIDTH_A = N_GROUPS * GROUP_W
N_HEADS_A = N_GROUPS * HEADS_PER_GROUP
STEPS_BACK = 128
A_BLOCK = 128
N_HEADS_B = 6
HEAD_DIM_B = 128
WIDTH_B = N_HEADS_B * HEAD_DIM_B
CONV_WIDTH = 4
CONV_DIM = 3 * WIDTH_B
DEEPNORM_ALPHA = (2 * DEPTH) ** 0.25
LN_EPS = 1e-5
RMS_EPS = 1e-6
LANES = 128
SUBLANES = 8
HALVES = GROUP_W // LANES

SRC_ZA = 3 * WIDTH_A
SRC_QKVB = SRC_ZA + WIDTH_A
SRC_ZB = SRC_QKVB + CONV_DIM
SRC_AB = SRC_ZB + WIDTH_B
SRC_GA = SRC_AB + 2 * N_HEADS_B
SRC_GB = SRC_GA + D_MODEL

AB_BLOCK = 3 * WIDTH_A // LANES
ATT_BLOCKS = AB_BLOCK + 1
ATT_W = ATT_BLOCKS * LANES
REST_ZA = 0
REST_QKVB = 768
REST_ZB = 3072
REST_GA = 4096
REST_GB = 5120
REST_W = 6144
REST_TILE = 2048

PROMPT_GDN_CHUNK = 128
PROMPT_GDN_SEQS = 2
TRI_BASE = 16
ATTN_TASKS = 2
ATTN_STATIC_BLOCKS = 4
PROMPT_ROW_TILE = 512
PROMPT_OUT_ROW_TILE = 256
SAMPLE_SEQ_TILE = 32
SAMPLE_GDN_TILING = (8, 4)
VMEM_LIMIT = 48 * 1024 * 1024

NT_DIMS = (((1,), (1,)), ((), ()))


def _cparams(sem):
    return pltpu.CompilerParams(dimension_semantics=sem, vmem_limit_bytes=VMEM_LIMIT)


def _sigmoid(x):
    return 0.5 * jnp.tanh(0.5 * x) + 0.5


def _silu(x):
    return x * _sigmoid(x)


def _bdot(a, b):
    return jnp.dot(a.astype(BF16), b.astype(BF16), preferred_element_type=F32)


def _bdot_nt(a, b):
    return lax.dot_general(a.astype(BF16), b.astype(BF16), NT_DIMS, preferred_element_type=F32)


def _mod_kernel(c_ref, w_ref, b_ref, o_ref):
    o_ref[0] = _bdot(_silu(c_ref[...]), w_ref[0]) + b_ref[0]


def _modulation(c_all, w_ada, b_ada):
    n = c_all.shape[0]
    return pl.pallas_call(
        _mod_kernel,
        out_shape=jax.ShapeDtypeStruct((DEPTH, n, 3 * D_MODEL), F32),
        grid=(DEPTH, 3),
        in_specs=[
            pl.BlockSpec((n, D_MODEL), lambda l, j: (0, 0)),
            pl.BlockSpec((1, D_MODEL, D_MODEL), lambda l, j: (l, 0, j)),
            pl.BlockSpec((1, 1, D_MODEL), lambda l, j: (l, 0, j)),
        ],
        out_specs=pl.BlockSpec((1, n, D_MODEL), lambda l, j: (l, 0, j)),
        compiler_params=_cparams(("arbitrary", "arbitrary")),
        name="adaln_modulation",
    )(c_all, w_ada, b_ada.reshape(DEPTH, 1, 3 * D_MODEL))


def _modulated(x_ref, mod_ref):
    bs, bt, _ = x_ref.shape
    shift = mod_ref[:, :, 0:D_MODEL]
    scale = mod_ref[:, :, D_MODEL:2 * D_MODEL]
    h = x_ref[...] * (1.0 + scale) + shift
    return h.reshape(bs * bt, D_MODEL).astype(BF16)


def _inproj_att_kernel(x_ref, mod_ref, w_ref, o_ref):
    bs, bt, _ = x_ref.shape
    res = jnp.dot(_modulated(x_ref, mod_ref), w_ref[0], preferred_element_type=F32)
    for j in range(ATT_BLOCKS):
        o_ref[:, j] = res[:, j * LANES:(j + 1) * LANES].reshape(bs, bt, LANES)


def _inproj_rest_kernel(x_ref, mod_ref, w_ref, o_ref):
    bs, bt, _ = x_ref.shape
    res = jnp.dot(_modulated(x_ref, mod_ref), w_ref[0], preferred_element_type=F32)
    o_ref[...] = res.reshape(bs, bt, REST_TILE).astype(o_ref.dtype)


def _in_projection(x, mod, w_att, w_rest, layer, bs, bt, rest_dtype):
    nseq, t, _ = x.shape
    att = pl.pallas_call(
        _inproj_att_kernel,
        out_shape=jax.ShapeDtypeStruct((nseq, ATT_BLOCKS, t, LANES), F32),
        grid=(nseq // bs, t // bt),
        in_specs=[
            pl.BlockSpec((bs, bt, D_MODEL), lambda i, k: (i, k, 0)),
            pl.BlockSpec((bs, 1, 3 * D_MODEL), lambda i, k: (i, 0, 0)),
            pl.BlockSpec((1, D_MODEL, ATT_W), lambda i, k: (layer, 0, 0)),
        ],
        out_specs=pl.BlockSpec((bs, ATT_BLOCKS, bt, LANES), lambda i, k: (i, 0, k, 0)),
        compiler_params=_cparams(("arbitrary", "arbitrary")),
        name="in_projection_att",
    )(x, mod, w_att)
    rest = pl.pallas_call(
        _inproj_rest_kernel,
        out_shape=jax.ShapeDtypeStruct((nseq, t, REST_W), rest_dtype),
        grid=(REST_W // REST_TILE, nseq // bs, t // bt),
        in_specs=[
            pl.BlockSpec((bs, bt, D_MODEL), lambda j, i, k: (i, k, 0)),
            pl.BlockSpec((bs, 1, 3 * D_MODEL), lambda j, i, k: (i, 0, 0)),
            pl.BlockSpec((1, D_MODEL, REST_TILE), lambda j, i, k: (layer, 0, j)),
        ],
        out_specs=pl.BlockSpec((bs, bt, REST_TILE), lambda j, i, k: (i, k, j)),
        compiler_params=_cparams(("arbitrary", "arbitrary", "arbitrary")),
        name="in_projection_rest",
    )(x, mod, w_rest)
    return att, rest


def _attn_prompt_kernel(slopes_ref, q_ref, kp_ref, ko_ref, vp_ref, vo_ref, o_ref, lse_ref, *, group, dilation, units,
                        tasks):
    n = pl.program_id(1)
    span = A_BLOCK * dilation
    qi = lax.broadcasted_iota(jnp.int32, (A_BLOCK, 2 * A_BLOCK), 0)
    kj = lax.broadcasted_iota(jnp.int32, (A_BLOCK, 2 * A_BLOCK), 1)
    steps = A_BLOCK + qi - kj
    band = (steps >= 0) & (steps <= STEPS_BACK)
    first_span = band & ((kj >= A_BLOCK) | (n > 0))
    dist = (steps * dilation).astype(F32)
    lane_head = lax.broadcasted_iota(jnp.int32, (1, GROUP_W), 1) // HEAD_DIM_A
    heads = range(HEADS_PER_GROUP)

    def rows_of(ref, rows):
        return jnp.concatenate([ref[0, half, rows, :] for half in range(HALVES)], axis=-1)

    def run(tasks):
        q, k, v, valid, own_rows = [], [], [], [], []
        for u, r in tasks:
            own = pl.ds(u * span + r, A_BLOCK, stride=dilation)
            if u == 0:
                prev_k, prev_v, prev = kp_ref, vp_ref, pl.ds(r, A_BLOCK, stride=dilation)
            else:
                prev_k, prev_v, prev = ko_ref, vo_ref, pl.ds((u - 1) * span + r, A_BLOCK, stride=dilation)
            own_rows.append(own)
            valid.append(first_span if u == 0 else band)
            q.append(rows_of(q_ref, own) * (HEAD_DIM_A ** -0.5))
            k.append(jnp.concatenate([rows_of(prev_k, prev), rows_of(ko_ref, own)], axis=0).astype(BF16))
            v.append(jnp.concatenate([rows_of(prev_v, prev), rows_of(vo_ref, own)], axis=0).astype(BF16))
        chains = [(t, h) for t in range(len(tasks)) for h in heads]
        s = [lax.dot_general(jnp.where(lane_head == h, q[t], 0.0).astype(BF16), k[t], NT_DIMS,
                             preferred_element_type=F32) for t, h in chains]
        s = [jnp.where(valid[t], s[c] - slopes_ref[group, h] * dist, -jnp.inf) for c, (t, h) in enumerate(chains)]
        m = [x.max(-1, keepdims=True) for x in s]
        p = [jnp.exp(x - mx) for x, mx in zip(s, m)]
        den = [x.sum(-1, keepdims=True) for x in p]
        oh = [jnp.dot((p[c] / den[c]).astype(BF16), v[t], preferred_element_type=F32) for c, (t, h) in enumerate(chains)]
        for t in range(len(tasks)):
            o_acc = jnp.zeros((A_BLOCK, GROUP_W), F32)
            lse_acc = jnp.zeros((A_BLOCK, GROUP_W), F32)
            for h in heads:
                c = t * HEADS_PER_GROUP + h
                o_acc = jnp.where(lane_head == h, oh[c], o_acc)
                lse_acc = jnp.where(lane_head == h, m[c] + jnp.log(den[c]), lse_acc)
            for half in range(HALVES):
                o_ref[0, half, own_rows[t], :] = o_acc[:, half * LANES:(half + 1) * LANES]
                lse_ref[0, half, own_rows[t], :] = lse_acc[:, half * LANES:(half + 1) * LANES]

    if dilation * units <= ATTN_STATIC_BLOCKS:
        blocks = [(u, r) for u in range(units) for r in range(dilation)]
        for j in range(0, len(blocks), tasks):
            run(blocks[j:j + tasks])
    else:
        for u in range(units):
            def residues(j, carry, u=u):
                run([(u, j * tasks + t) for t in range(tasks)])
                return carry
            lax.fori_loop(0, dilation // tasks, residues, 0)


def _attention_prompt(att, slopes, group):
    b, _, s, _ = att.shape
    d = DILATIONS[group]
    span = A_BLOCK * d
    units = max(1, PROMPT_ROW_TILE // span)
    rows = span * units
    tasks = ATTN_TASKS * (2 if d > SUBLANES else 1)
    qb, kb, vb = group, N_GROUPS + group, 2 * N_GROUPS + group
    own = lambda col: pl.BlockSpec((1, HALVES, rows, LANES), lambda i, n: (i, col, n, 0))
    prev = lambda col: pl.BlockSpec((1, HALVES, span, LANES), lambda i, n: (i, col, jnp.maximum(n * units - 1, 0), 0))
    out_spec = pl.BlockSpec((1, HALVES, rows, LANES), lambda i, n: (i, 0, n, 0))
    return pl.pallas_call(
        functools.partial(_attn_prompt_kernel, group=group, dilation=d, units=units, tasks=tasks),
        out_shape=[jax.ShapeDtypeStruct((b, HALVES, s, LANES), F32)] * 2,
        grid=(b, s // rows),
        in_specs=[pl.BlockSpec(memory_space=pltpu.SMEM), own(qb), prev(kb), own(kb), prev(vb), own(vb)],
        out_specs=[out_spec, out_spec],
        compiler_params=_cparams(("arbitrary", "arbitrary")),
        name=f"attn_prompt_g{group}",
    )(slopes, att, att, att, att, att)


def _sample_bias(slopes, group, t_new):
    d = DILATIONS[group]
    lb = WINDOWS[group]
    t = np.arange(t_new)

    def table(key_pos, width):
        diff = (lb + t)[:, None] - key_pos[None, :]
        ok = (diff >= 0) & (diff % d == 0) & (diff // d <= STEPS_BACK)
        dist = jnp.asarray(np.where(ok, diff, 0), F32)
        bias = jnp.where(jnp.asarray(ok)[None], -slopes[group][:, None, None] * dist[None], -jnp.inf)
        bias = bias.reshape(HEADS_PER_GROUP * t_new, key_pos.shape[0])
        return jnp.pad(bias, ((0, 0), (0, width - key_pos.shape[0])), constant_values=-jnp.inf)

    return table(np.arange(lb), lb), table(lb + t, LANES)


def _attn_sample_kernel(q_ref, k_ref, v_ref, cache_ref, bias_ref, biasn_ref, o_ref, lse_ref, *, bb, per_iter, t_new):
    nq = HEADS_PER_GROUP * t_new
    row_head = lax.broadcasted_iota(jnp.int32, (nq, GROUP_W), 0) // t_new
    lane_head2 = lax.broadcasted_iota(jnp.int32, (nq, GROUP_W), 1) // HEAD_DIM_A
    lane_head = lax.broadcasted_iota(jnp.int32, (1, GROUP_W), 1) // HEAD_DIM_A
    pad_n = jnp.zeros((LANES - t_new, GROUP_W), F32)

    def new_rows(ref, i):
        return jnp.concatenate([ref[i, half] for half in range(HALVES)], axis=-1)

    def body(it, carry):
        seqs = [it * per_iter + s for s in range(per_iter)]
        each = lambda f: [f(i) for i in seqs]
        n = range(per_iter)
        q = each(lambda i: new_rows(q_ref, i) * (HEAD_DIM_A ** -0.5))
        qm = [jnp.where(row_head == lane_head2, jnp.concatenate([x] * HEADS_PER_GROUP, axis=0), 0.0).astype(BF16)
              for x in q]
        kt = each(lambda i: cache_ref[0, i, 0].astype(BF16))
        vt = each(lambda i: cache_ref[0, i, 1].astype(BF16))
        kn = each(lambda i: jnp.concatenate([new_rows(k_ref, i), pad_n], axis=0).astype(BF16))
        vn = each(lambda i: jnp.concatenate([new_rows(v_ref, i), pad_n], axis=0).astype(BF16))
        sc = [jnp.dot(qm[c], kt[c], preferred_element_type=F32) + bias_ref[...] for c in n]
        sn = [lax.dot_general(qm[c], kn[c], NT_DIMS, preferred_element_type=F32) + biasn_ref[...] for c in n]
        m = [jnp.maximum(sc[c].max(-1, keepdims=True), sn[c].max(-1, keepdims=True)) for c in n]
        pc = [jnp.exp(sc[c] - m[c]) for c in n]
        pn = [jnp.exp(sn[c] - m[c]) for c in n]
        den = [pc[c].sum(-1, keepdims=True) + pn[c].sum(-1, keepdims=True) for c in n]
        of = [lax.dot_general((pc[c] / den[c]).astype(BF16), vt[c], NT_DIMS, preferred_element_type=F32)
              + jnp.dot((pn[c] / den[c]).astype(BF16), vn[c], preferred_element_type=F32) for c in n]
        for c, i in enumerate(seqs):
            lse_col = m[c] + jnp.log(den[c])
            o = jnp.zeros((t_new, GROUP_W), F32)
            lse = jnp.zeros((t_new, GROUP_W), F32)
            for h in range(HEADS_PER_GROUP):
                head = lane_head == h
                o = jnp.where(head, of[c][h * t_new:(h + 1) * t_new, :], o)
                lse = jnp.where(head, lse_col[h * t_new:(h + 1) * t_new, :], lse)
            for half in range(HALVES):
                o_ref[i, half] = o[:, half * LANES:(half + 1) * LANES]
                lse_ref[i, half] = lse[:, half * LANES:(half + 1) * LANES]
        return carry

    lax.fori_loop(0, bb // per_iter, body, 0)


def _attention_sample(att, cache_t, layer, slopes, group, bb, per_iter):
    nseq, _, t_new, _ = att.shape
    lb = WINDOWS[group]
    bias, bias_new = _sample_bias(slopes, group, t_new)
    qb, kb, vb = group, N_GROUPS + group, 2 * N_GROUPS + group
    new_spec = lambda col: pl.BlockSpec((bb, HALVES, t_new, LANES), lambda i: (i, col, 0, 0))
    out_spec = pl.BlockSpec((bb, HALVES, t_new, LANES), lambda i: (i, 0, 0, 0))
    return pl.pallas_call(
        functools.partial(_attn_sample_kernel, bb=bb, per_iter=per_iter, t_new=t_new),
        out_shape=[jax.ShapeDtypeStruct((nseq, HALVES, t_new, LANES), F32)] * 2,
        grid=(nseq // bb,),
        in_specs=[new_spec(qb), new_spec(kb), new_spec(vb),
                  pl.BlockSpec((1, bb, 2, GROUP_W, lb), lambda i: (layer, i, 0, 0, 0)),
                  pl.BlockSpec(bias.shape, lambda i: (0, 0)), pl.BlockSpec(bias_new.shape, lambda i: (0, 0))],
        out_specs=[out_spec, out_spec],
        compiler_params=_cparams(("arbitrary",)),
        name=f"attn_sample_g{group}",
    )(att, att, att, cache_t, bias, bias_new)


def _unit_lower_inverses(lowers, ri, ci):
    c = lowers[0].shape[0]
    base = min(TRI_BASE, c)
    ident = (ri == ci).astype(F32)
    same = (ri // base) == (ci // base)
    powers = [jnp.where(same, l, 0.0) for l in lowers]
    invs = [ident - p for p in powers]
    for _ in range(int(np.log2(base)) - 1):
        powers = [_bdot(p, p) for p in powers]
        invs = [t + _bdot(t, p) for t, p in zip(invs, powers)]
    size = base
    while size < c:
        pair = ((ri // (2 * size)) == (ci // (2 * size))) & ((ri // size) % 2 == 1) & ((ci // size) % 2 == 0)
        offs = [_bdot(jnp.where(pair, l, 0.0), t) for l, t in zip(lowers, invs)]
        invs = [t - _bdot(t, o) for t, o in zip(invs, offs)]
        size *= 2
    return invs


def _gdn_kernel(*refs, bb, per_iter, chunk, layered, single_chunk, aliased):
    (q_ref, k_ref, v_ref, ab_ref, conv0_ref, s0_ref, cw_ref, par_ref), refs = refs[:8], refs[8 + int(aliased):]
    o_ref, s_ref, ext_ref = refs
    c_idx = pl.program_id(1)
    C = chunk
    first_row = SUBLANES - (CONV_WIDTH - 1)
    if layered:
        s0_ref, s_ref = s0_ref.at[0], s_ref.at[0]
    cw_ref, par_ref = cw_ref.at[0], par_ref.at[0]
    s_prev = s0_ref if single_chunk else s_ref

    @pl.when(c_idx == 0)
    def _():
        ext_ref[:, 0:SUBLANES, :] = conv0_ref[...]
        if not single_chunk:
            s_ref[...] = s0_ref[...]

    @pl.when(c_idx > 0)
    def _():
        ext_ref[:, 0:SUBLANES, :] = ext_ref[:, C:C + SUBLANES, :]

    ext_ref[:, SUBLANES:SUBLANES + C, 0:WIDTH_B] = q_ref[...].astype(F32)
    ext_ref[:, SUBLANES:SUBLANES + C, WIDTH_B:2 * WIDTH_B] = k_ref[...].astype(F32)
    ext_ref[:, SUBLANES:SUBLANES + C, 2 * WIDTH_B:3 * WIDTH_B] = v_ref[...].astype(F32)

    ri = lax.broadcasted_iota(jnp.int32, (C, C), 0)
    ci = lax.broadcasted_iota(jnp.int32, (C, C), 1)
    causal = ri >= ci
    strict = ri > ci
    tril = causal.astype(F32)
    neg_a = -jnp.exp(par_ref[0:1, :])
    dt_bias = par_ref[1:2, :]
    norm_w = par_ref[2:3, :]
    pad_rows = jnp.zeros((HEAD_DIM_B - C, HEAD_DIM_B), F32) if C < HEAD_DIM_B else None

    def pad128(x):
        return x if pad_rows is None else jnp.concatenate([x, pad_rows], axis=0)

    def conv(i, col):
        acc = ext_ref[i, pl.ds(first_row, C), pl.ds(col, HEAD_DIM_B)] * cw_ref[0:1, pl.ds(col, HEAD_DIM_B)]
        for j in range(1, CONV_WIDTH):
            acc = acc + (ext_ref[i, pl.ds(first_row + j, C), pl.ds(col, HEAD_DIM_B)]
                         * cw_ref[j:j + 1, pl.ds(col, HEAD_DIM_B)])
        return _silu(acc)

    def l2n(x):
        return x * lax.rsqrt(jnp.sum(x * x, axis=-1, keepdims=True) + RMS_EPS)

    def gates(i):
        ab = ab_ref[i, 0]
        z = ab + dt_bias
        softplus = jnp.maximum(z, 0.0) + jnp.log(1.0 + jnp.exp(-jnp.abs(z)))
        g_all = neg_a * softplus
        beta_all = _sigmoid(ab)
        gcum_all = jnp.dot(tril, g_all, preferred_element_type=F32, precision=HIGHEST)
        return beta_all, gcum_all, pad128(gcum_all).T

    def body(it, carry):
        seqs = [it * per_iter + s for s in range(per_iter)]
        beta_all, gcum_all, gcum_rows = zip(*[gates(i) for i in seqs])
        chains = [(s, h) for s in range(per_iter) for h in range(N_HEADS_B)]
        each = lambda f: [f(s, h) for s, h in chains]
        n = range(len(chains))
        states = each(lambda s, h: s_prev[seqs[s], h])
        q = each(lambda s, h: l2n(conv(seqs[s], h * HEAD_DIM_B)) * (HEAD_DIM_B ** -0.5))
        k = each(lambda s, h: l2n(conv(seqs[s], WIDTH_B + h * HEAD_DIM_B)))
        v = each(lambda s, h: conv(seqs[s], 2 * WIDTH_B + h * HEAD_DIM_B))
        beta = each(lambda s, h: beta_all[s][:, N_HEADS_B + h:N_HEADS_B + h + 1])
        gcum = each(lambda s, h: gcum_all[s][:, h:h + 1])
        grow = each(lambda s, h: gcum_rows[s][h:h + 1, 0:C])
        decay = [jnp.where(causal, jnp.exp(jnp.minimum(gcum[c] - grow[c], 0.0)), 0.0) for c in n]
        qk = [_bdot_nt(jnp.concatenate([q[c], k[c]], axis=0), k[c]) for c in n]
        intra = [qk[c][0:C] * decay[c] for c in n]
        lower = [jnp.where(strict, qk[c][C:2 * C] * beta[c] * decay[c], 0.0) for c in n]
        inv = _unit_lower_inverses(lower, ri, ci)
        e_g = [jnp.exp(gcum[c]) for c in n]
        uw = [_bdot(inv[c], jnp.concatenate([v[c] * beta[c], k[c] * beta[c] * e_g[c]], axis=-1)) for c in n]
        ws_qs = [_bdot(jnp.concatenate([uw[c][:, HEAD_DIM_B:2 * HEAD_DIM_B], q[c] * e_g[c]], axis=0), states[c])
                 for c in n]
        v_new = [uw[c][:, 0:HEAD_DIM_B] - ws_qs[c][0:C] for c in n]
        g_last = [gcum[c][C - 1:C, :] for c in n]
        k_dec = [k[c] * jnp.exp(g_last[c] - gcum[c]) for c in n]
        o = [ws_qs[c][C:2 * C] + _bdot(intra[c], v_new[c]) for c in n]
        new_states = [states[c] * jnp.exp(g_last[c]) + _bdot(pad128(k_dec[c]).T, pad128(v_new[c])) for c in n]
        for c, (s, h) in enumerate(chains):
            s_ref[seqs[s], h] = new_states[c]
            o_c = o[c] * lax.rsqrt(jnp.mean(o[c] * o[c], axis=-1, keepdims=True) + RMS_EPS) * norm_w
            o_ref[seqs[s], :, pl.ds(h * HEAD_DIM_B, HEAD_DIM_B)] = o_c
        return carry

    lax.fori_loop(0, bb // per_iter, body, 0)


def _gated_deltanet(rest, att, conv0, s0, layer, conv_w, params, bb, per_iter, chunk, state_buf=None):
    nseq, t, _ = rest.shape
    qb = REST_QKVB // WIDTH_B
    col_spec = lambda col: pl.BlockSpec((bb, chunk, WIDTH_B), lambda i, c: (i, c, col))
    state_shape = (bb, N_HEADS_B, HEAD_DIM_B, HEAD_DIM_B)
    layered = s0.ndim == 5
    if layered:
        state_spec = pl.BlockSpec((1,) + state_shape, lambda i, c: (layer, i, 0, 0, 0))
        state_out = jax.ShapeDtypeStruct(s0.shape, F32)
    else:
        state_spec = pl.BlockSpec(state_shape, lambda i, c: (i, 0, 0, 0))
        state_out = jax.ShapeDtypeStruct((nseq,) + state_shape[1:], F32)
    in_specs = [col_spec(qb), col_spec(qb + 1), col_spec(qb + 2),
                pl.BlockSpec((bb, 1, chunk, LANES), lambda i, c: (i, AB_BLOCK, c, 0)),
                pl.BlockSpec((bb, SUBLANES, CONV_DIM), lambda i, c: (i, 0, 0)),
                state_spec,
                pl.BlockSpec((1, SUBLANES, CONV_DIM), lambda i, c: (layer, 0, 0)),
                pl.BlockSpec((1, SUBLANES, LANES), lambda i, c: (layer, 0, 0))]
    args = [rest, rest, rest, att, conv0, s0, conv_w, params]
    aliases = {}
    if state_buf is not None:
        aliases = {len(args): 1}
        in_specs.append(pl.BlockSpec(memory_space=pl.ANY))
        args.append(state_buf)
    return pl.pallas_call(
        functools.partial(_gdn_kernel, bb=bb, per_iter=per_iter, chunk=chunk, layered=layered,
                          single_chunk=(t == chunk), aliased=state_buf is not None),
        out_shape=[jax.ShapeDtypeStruct((nseq, t, WIDTH_B), F32), state_out],
        grid=(nseq // bb, t // chunk),
        in_specs=in_specs,
        out_specs=[pl.BlockSpec((bb, chunk, WIDTH_B), lambda i, c: (i, c, 0)), state_spec],
        scratch_shapes=[pltpu.VMEM((bb, chunk + SUBLANES, CONV_DIM), F32)],
        input_output_aliases=aliases,
        compiler_params=_cparams(("arbitrary", "arbitrary")),
        name="gated_deltanet",
    )(*args)


def _out_kernel(x_ref, mod_ref, o1_ref, o2_ref, o3_ref, l1_ref, l2_ref, l3_ref, za_ref, ob_ref, zb_ref, ga_ref, gb_ref,
                wa_ref, wb_ref, wo_ref, lng_ref, lnb_ref, y_ref):
    bs, bt, _ = x_ref.shape
    rows = bs * bt
    flat = lambda ref: ref[...].reshape(rows, ref.shape[-1]).astype(F32)
    group = lambda ref: jnp.concatenate([ref[:, half].reshape(rows, LANES) for half in range(HALVES)], axis=-1)
    l1, l2, l3 = group(l1_ref), group(l2_ref), group(l3_ref)
    m = jnp.maximum(jnp.maximum(l1, l2), l3)
    e1, e2, e3 = jnp.exp(l1 - m), jnp.exp(l2 - m), jnp.exp(l3 - m)
    inv_tot = 1.0 / (e1 + e2 + e3)
    att = jnp.concatenate([group(o1_ref) * (e1 * inv_tot), group(o2_ref) * (e2 * inv_tot),
                           group(o3_ref) * (e3 * inv_tot)], axis=-1)
    branch_a = jnp.dot((att * _silu(flat(za_ref))).astype(BF16), wa_ref[0], preferred_element_type=F32)
    branch_b = jnp.dot((flat(ob_ref) * _silu(flat(zb_ref))).astype(BF16), wb_ref[0], preferred_element_type=F32)
    merged = _sigmoid(flat(ga_ref)) * branch_a + _sigmoid(flat(gb_ref)) * branch_b
    out = jnp.dot(merged.astype(BF16), wo_ref[0], preferred_element_type=F32).reshape(bs, bt, D_MODEL)
    gate = mod_ref[:, :, 2 * D_MODEL:3 * D_MODEL]
    y = DEEPNORM_ALPHA * x_ref[...] + gate * out
    mu = jnp.mean(y, axis=-1, keepdims=True)
    yc = y - mu
    var = jnp.mean(yc * yc, axis=-1, keepdims=True)
    y_ref[...] = yc * lax.rsqrt(var + LN_EPS) * lng_ref[...] + lnb_ref[...]


def _output_block(x, mod, o_g, lse_g, o_b, rest, wa, wb, wo, ln_g, ln_b, layer, bs, bt):
    nseq, t, _ = x.shape
    row = lambda w, col: pl.BlockSpec((bs, bt, w), lambda i, k: (i, k, col))
    halves = pl.BlockSpec((bs, HALVES, bt, LANES), lambda i, k: (i, 0, k, 0))
    per_layer = lambda shape: pl.BlockSpec((1,) + shape, lambda i, k: (layer,) + (0,) * len(shape))
    return pl.pallas_call(
        _out_kernel,
        out_shape=jax.ShapeDtypeStruct((nseq, t, D_MODEL), F32),
        grid=(nseq // bs, t // bt),
        in_specs=[row(D_MODEL, 0),
                  pl.BlockSpec((bs, 1, 3 * D_MODEL), lambda i, k: (i, 0, 0)),
                  halves, halves, halves, halves, halves, halves,
                  row(WIDTH_A, REST_ZA // WIDTH_A),
                  row(WIDTH_B, 0),
                  row(WIDTH_B, REST_ZB // WIDTH_B),
                  row(D_MODEL, REST_GA // D_MODEL),
                  row(D_MODEL, REST_GB // D_MODEL),
                  per_layer((WIDTH_A, D_MODEL)), per_layer((WIDTH_B, D_MODEL)), per_layer((D_MODEL, D_MODEL)),
                  per_layer((1, D_MODEL)), per_layer((1, D_MODEL))],
        out_specs=row(D_MODEL, 0),
        compiler_params=_cparams(("arbitrary", "arbitrary")),
        name="output_block",
    )(x, mod, o_g[0], o_g[1], o_g[2], lse_g[0], lse_g[1], lse_g[2], rest, o_b, rest, rest, rest,
      wa, wb, wo, ln_g, ln_b)


def _alibi_slopes():
    i = jnp.arange(1, N_HEADS_A + 1, dtype=F32)
    return jnp.exp2(-8.0 * i / N_HEADS_A).reshape(N_GROUPS, HEADS_PER_GROUP)


def _split_w_in(w_in):
    zeros = lambda n: jnp.zeros(w_in.shape[:2] + (n,), BF16)
    att = jnp.concatenate([w_in[:, :, :SRC_ZA].astype(BF16), w_in[:, :, SRC_AB:SRC_GA].astype(BF16),
                           zeros(LANES - 2 * N_HEADS_B)], axis=-1)
    rest = jnp.concatenate([w_in[:, :, SRC_ZA:SRC_AB].astype(BF16), zeros(REST_GA - (SRC_AB - SRC_ZA)),
                            w_in[:, :, SRC_GA:].astype(BF16)], axis=-1)
    return att, rest


def _split_kv(att, group, start):
    b = att.shape[0]
    k = att[:, (N_GROUPS + group) * HALVES:(N_GROUPS + group + 1) * HALVES, start:, :]
    v = att[:, (2 * N_GROUPS + group) * HALVES:(2 * N_GROUPS + group + 1) * HALVES, start:, :]
    kv = jnp.stack([k, v], axis=1)
    kv = kv.transpose(0, 3, 1, 2, 4)
    return kv.reshape(b, kv.shape[1], 2, HEADS_PER_GROUP, HEAD_DIM_A)


def _feature_major(cache):
    d, b, l = cache.shape[:3]
    return cache.transpose(0, 1, 3, 4, 5, 2).reshape(d, b, 2, GROUP_W, l)


def kernel(x_prompt, x_sample, c_prompt, c_sample, cache_win1, cache_win2, cache_win3, state_gdn, state_conv,
           w_ada, b_ada, w_in, conv_w, a_log, dt_bias, gdn_norm, w_out_a, w_out_b, w_o, ln_g, ln_b):
    n_p, seq, _ = x_prompt.shape
    n_s, t_new, _ = x_sample.shape
    caches = tuple(_feature_major(c) for c in (cache_win1, cache_win2, cache_win3))
    sample_attn_tiling = ((8, 4), (8, 2), (2, 2))
    slopes = _alibi_slopes()

    w_att, w_rest = _split_w_in(w_in)
    wa = w_out_a.astype(BF16)
    wb = w_out_b.astype(BF16)
    wo = w_o.astype(BF16)
    ln_g3 = ln_g.reshape(DEPTH, 1, D_MODEL)
    ln_b3 = ln_b.reshape(DEPTH, 1, D_MODEL)
    conv_w_p = jnp.pad(conv_w, ((0, 0), (0, SUBLANES - CONV_WIDTH), (0, 0)))
    params = jnp.zeros((DEPTH, SUBLANES, LANES), F32)
    params = params.at[:, 0, :N_HEADS_B].set(a_log).at[:, 1, :N_HEADS_B].set(dt_bias).at[:, 2, :].set(gdn_norm)

    n_all = n_p + n_s
    n_pad = -(-n_all // SUBLANES) * SUBLANES
    c_all = jnp.pad(jnp.concatenate([c_prompt, c_sample], axis=0), ((0, n_pad - n_all), (0, 0)))
    mod_all = _modulation(c_all, w_ada, b_ada)

    zero_conv = jnp.zeros((n_p, SUBLANES, CONV_DIM), F32)
    zero_state = jnp.zeros((n_p, N_HEADS_B, HEAD_DIM_B, HEAD_DIM_B), F32)
    conv_s = jnp.pad(state_conv, ((0, 0), (0, 0), (SUBLANES - (CONV_WIDTH - 1), 0), (0, 0)))

    xp, xs = x_prompt, x_sample
    acc_p = [[] for _ in range(N_GROUPS + 2)]
    acc_s = [[] for _ in range(N_GROUPS + 2)]
    gdn_sample = None
    for l in range(DEPTH):
        mod_p = mod_all[l, :n_p, None, :]
        mod_s = mod_all[l, n_p:n_all, None, :]

        att_p, rest_p = _in_projection(xp, mod_p, w_att, w_rest, l, 1, PROMPT_ROW_TILE, BF16)
        att_s, rest_s = _in_projection(xs, mod_s, w_att, w_rest, l, SAMPLE_SEQ_TILE, t_new, F32)

        o_p, lse_p, o_s, lse_s = [], [], [], []
        for g in range(N_GROUPS):
            o, lse = _attention_prompt(att_p, slopes, g)
            o_p.append(o)
            lse_p.append(lse)
            o, lse = _attention_sample(att_s, caches[g], l, slopes, g, *sample_attn_tiling[g])
            o_s.append(o)
            lse_s.append(lse)

        ob_p, st_p = _gated_deltanet(rest_p, att_p, zero_conv, zero_state, l, conv_w_p, params, PROMPT_GDN_SEQS,
                                     PROMPT_GDN_SEQS, PROMPT_GDN_CHUNK)
        ob_s, gdn_sample = _gated_deltanet(rest_s, att_s, conv_s[l], state_gdn, l, conv_w_p, params,
                                           *SAMPLE_GDN_TILING, t_new, state_buf=gdn_sample)

        xp_new = _output_block(xp, mod_p, o_p, lse_p, ob_p, rest_p, wa, wb, wo, ln_g3, ln_b3, l,
                               1, PROMPT_OUT_ROW_TILE)
        xs_new = _output_block(xs, mod_s, o_s, lse_s, ob_s, rest_s, wa, wb, wo, ln_g3, ln_b3, l,
                               SAMPLE_SEQ_TILE, t_new)

        for g in range(N_GROUPS):
            acc_p[g].append(_split_kv(att_p, g, seq - min(WINDOWS[g], seq)))
            acc_s[g].append(_split_kv(att_s, g, 0))
        acc_p[N_GROUPS].append(st_p)
        acc_p[N_GROUPS + 1].append(rest_p[:, seq - (CONV_WIDTH - 1):, REST_QKVB:REST_QKVB + CONV_DIM].astype(F32))
        acc_s[N_GROUPS + 1].append(rest_s[:, t_new - (CONV_WIDTH - 1):, REST_QKVB:REST_QKVB + CONV_DIM])
        xp, xs = xp_new, xs_new

    acc_s[N_GROUPS] = None
    outs_p = [jnp.stack(a) for a in acc_p]
    outs_s = [gdn_sample if a is None else jnp.stack(a) for a in acc_s]
    return (xp, xs, *outs_p, *outs_s)
```

```python
import functools

import numpy as np
import jax
import jax.numpy as jnp
from jax import lax
from jax.experimental import pallas as pl
from jax.experimental.pallas import tpu as pltpu

F32 = jnp.float32
BF16 = jnp.bfloat16
HIGHEST = lax.Precision.HIGHEST

D_MODEL = 1024
DEPTH = 4
DILATIONS = (1, 4, 16)
WINDOWS = (128, 512, 2048)
N_GROUPS = 3
HEADS_PER_GROUP = 4
HEAD_DIM_A = 64
GROUP_W = HEADS_PER_GROUP * HEAD_DIM_A
WIDTH_A = N_GROUPS * GROUP_W
N_HEADS_A = N_GROUPS * HEADS_PER_GROUP
STEPS_BACK = 128
A_BLOCK = 128
N_HEADS_B = 6
HEAD_DIM_B = 128
WIDTH_B = N_HEADS_B * HEAD_DIM_B
CONV_WIDTH = 4
CONV_DIM = 3 * WIDTH_B
DEEPNORM_ALPHA = (2 * DEPTH) ** 0.25
LN_EPS = 1e-5
RMS_EPS = 1e-6
LANES = 128
SUBLANES = 8
HALVES = GROUP_W // LANES

SRC_ZA = 3 * WIDTH_A
SRC_QKVB = SRC_ZA + WIDTH_A
SRC_ZB = SRC_QKVB + CONV_DIM
SRC_AB = SRC_ZB + WIDTH_B
SRC_GA = SRC_AB + 2 * N_HEADS_B
SRC_GB = SRC_GA + D_MODEL

AB_BLOCK = 3 * WIDTH_A // LANES
ATT_BLOCKS = AB_BLOCK + 1
ATT_W = ATT_BLOCKS * LANES
REST_ZA = 0
REST_QKVB = 768
REST_ZB = 3072
REST_GA = 4096
REST_GB = 5120
REST_W = 6144
REST_TILE = 2048

PROMPT_GDN_CHUNK = 128
PROMPT_GDN_SEQS = 2
TRI_BASE = 16
ATTN_TASKS = 2
ATTN_STATIC_BLOCKS = 4
PROMPT_ROW_TILE = 512
PROMPT_OUT_ROW_TILE = 256
SAMPLE_SEQ_TILE = 32
SAMPLE_GDN_TILING = (8, 4)
VMEM_LIMIT = 48 * 1024 * 1024

NT_DIMS = (((1,), (1,)), ((), ()))


def _cparams(sem):
    return pltpu.CompilerParams(dimension_semantics=sem, vmem_limit_bytes=VMEM_LIMIT)


def _sigmoid(x):
    return 0.5 * jnp.tanh(0.5 * x) + 0.5


def _silu_of_half(h):
    return h + h * jnp.tanh(h)


def _silu(x):
    return _silu_of_half(0.5 * x)


def _bdot(a, b):
    return jnp.dot(a.astype(BF16), b.astype(BF16), preferred_element_type=F32)


def _bdot_nt(a, b):
    return lax.dot_general(a.astype(BF16), b.astype(BF16), NT_DIMS, preferred_element_type=F32)


def _mod_kernel(c_ref, w_ref, b_ref, o_ref):
    o_ref[0] = _bdot(_silu(c_ref[...]), w_ref[0]) + b_ref[0]


def _modulation(c_all, w_ada, b_ada):
    n = c_all.shape[0]
    return pl.pallas_call(
        _mod_kernel,
        out_shape=jax.ShapeDtypeStruct((DEPTH, n, 3 * D_MODEL), F32),
        grid=(DEPTH, 3),
        in_specs=[
            pl.BlockSpec((n, D_MODEL), lambda l, j: (0, 0)),
            pl.BlockSpec((1, D_MODEL, D_MODEL), lambda l, j: (l, 0, j)),
            pl.BlockSpec((1, 1, D_MODEL), lambda l, j: (l, 0, j)),
        ],
        out_specs=pl.BlockSpec((1, n, D_MODEL), lambda l, j: (l, 0, j)),
        compiler_params=_cparams(("arbitrary", "arbitrary")),
        name="adaln_modulation",
    )(c_all, w_ada, b_ada.reshape(DEPTH, 1, 3 * D_MODEL))


def _modulated(x_ref, mod_ref):
    bs, bt, _ = x_ref.shape
    shift = mod_ref[:, :, 0:D_MODEL]
    scale = mod_ref[:, :, D_MODEL:2 * D_MODEL]
    h = x_ref[...] * (1.0 + scale) + shift
    return h.reshape(bs * bt, D_MODEL).astype(BF16)


def _inproj_att_kernel(x_ref, mod_ref, w_ref, o_ref):
    bs, bt, _ = x_ref.shape
    res = jnp.dot(_modulated(x_ref, mod_ref), w_ref[0], preferred_element_type=F32)
    for j in range(ATT_BLOCKS):
        o_ref[:, j] = res[:, j * LANES:(j + 1) * LANES].reshape(bs, bt, LANES)


def _inproj_rest_kernel(x_ref, mod_ref, w_ref, o_ref):
    bs, bt, _ = x_ref.shape
    res = jnp.dot(_modulated(x_ref, mod_ref), w_ref[0], preferred_element_type=F32)
    o_ref[...] = res.reshape(bs, bt, REST_TILE).astype(o_ref.dtype)


def _in_projection(x, mod, w_att, w_rest, layer, bs, bt, rest_dtype):
    nseq, t, _ = x.shape
    att = pl.pallas_call(
        _inproj_att_kernel,
        out_shape=jax.ShapeDtypeStruct((nseq, ATT_BLOCKS, t, LANES), F32),
        grid=(nseq // bs, t // bt),
        in_specs=[
            pl.BlockSpec((bs, bt, D_MODEL), lambda i, k: (i, k, 0)),
            pl.BlockSpec((bs, 1, 3 * D_MODEL), lambda i, k: (i, 0, 0)),
            pl.BlockSpec((1, D_MODEL, ATT_W), lambda i, k: (layer, 0, 0)),
        ],
        out_specs=pl.BlockSpec((bs, ATT_BLOCKS, bt, LANES), lambda i, k: (i, 0, k, 0)),
        compiler_params=_cparams(("arbitrary", "arbitrary")),
        name="in_projection_att",
    )(x, mod, w_att)
    rest = pl.pallas_call(
        _inproj_rest_kernel,
        out_shape=jax.ShapeDtypeStruct((nseq, t, REST_W), rest_dtype),
        grid=(REST_W // REST_TILE, nseq // bs, t // bt),
        in_specs=[
            pl.BlockSpec((bs, bt, D_MODEL), lambda j, i, k: (i, k, 0)),
            pl.BlockSpec((bs, 1, 3 * D_MODEL), lambda j, i, k: (i, 0, 0)),
            pl.BlockSpec((1, D_MODEL, REST_TILE), lambda j, i, k: (layer, 0, j)),
        ],
        out_specs=pl.BlockSpec((bs, bt, REST_TILE), lambda j, i, k: (i, k, j)),
        compiler_params=_cparams(("arbitrary", "arbitrary", "arbitrary")),
        name="in_projection_rest",
    )(x, mod, w_rest)
    return att, rest


def _attn_prompt_kernel(slopes_ref, q_ref, kp_ref, ko_ref, vp_ref, vo_ref, o_ref, lse_ref, *, group, dilation, units,
                        tasks):
    n = pl.program_id(1)
    span = A_BLOCK * dilation
    qi = lax.broadcasted_iota(jnp.int32, (A_BLOCK, 2 * A_BLOCK), 0)
    kj = lax.broadcasted_iota(jnp.int32, (A_BLOCK, 2 * A_BLOCK), 1)
    steps = A_BLOCK + qi - kj
    band = (steps >= 0) & (steps <= STEPS_BACK)
    dist = (steps * dilation).astype(F32)
    alibi = [jnp.where(band, -slopes_ref[group, h] * dist, -jnp.inf) for h in range(HEADS_PER_GROUP)]
    no_prev = jnp.where((kj >= A_BLOCK) | (n > 0), 0.0, -jnp.inf)
    lane_head = lax.broadcasted_iota(jnp.int32, (1, GROUP_W), 1) // HEAD_DIM_A
    heads = range(HEADS_PER_GROUP)

    def rows_of(ref, rows):
        return jnp.concatenate([ref[0, half, rows, :] for half in range(HALVES)], axis=-1)

    def run(tasks):
        q, k, v, own_rows = [], [], [], []
        for u, r in tasks:
            own = pl.ds(u * span + r, A_BLOCK, stride=dilation)
            if u == 0:
                prev_k, prev_v, prev = kp_ref, vp_ref, pl.ds(r, A_BLOCK, stride=dilation)
            else:
                prev_k, prev_v, prev = ko_ref, vo_ref, pl.ds((u - 1) * span + r, A_BLOCK, stride=dilation)
            own_rows.append(own)
            q.append(rows_of(q_ref, own) * (HEAD_DIM_A ** -0.5))
            k.append(jnp.concatenate([rows_of(prev_k, prev), rows_of(ko_ref, own)], axis=0).astype(BF16))
            v.append(jnp.concatenate([rows_of(prev_v, prev), rows_of(vo_ref, own)], axis=0).astype(BF16))
        chains = [(t, h) for t in range(len(tasks)) for h in heads]
        s = [lax.dot_general(jnp.where(lane_head == h, q[t], 0.0).astype(BF16), k[t], NT_DIMS,
                             preferred_element_type=F32) for t, h in chains]
        s = [s[c] + alibi[h] for c, (t, h) in enumerate(chains)]
        s = [s[c] + no_prev if tasks[t][0] == 0 else s[c] for c, (t, h) in enumerate(chains)]
        m = [x.max(-1, keepdims=True) for x in s]
        p = [jnp.exp(x - mx) for x, mx in zip(s, m)]
        den = [x.sum(-1, keepdims=True) for x in p]
        oh = [jnp.dot((p[c] / den[c]).astype(BF16), v[t], preferred_element_type=F32) for c, (t, h) in enumerate(chains)]
        for t in range(len(tasks)):
            o_acc = jnp.zeros((A_BLOCK, GROUP_W), F32)
            lse_acc = jnp.zeros((A_BLOCK, GROUP_W), F32)
            for h in heads:
                c = t * HEADS_PER_GROUP + h
                o_acc = jnp.where(lane_head == h, oh[c], o_acc)
                lse_acc = jnp.where(lane_head == h, m[c] + jnp.log(den[c]), lse_acc)
            for half in range(HALVES):
                o_ref[0, half, own_rows[t], :] = o_acc[:, half * LANES:(half + 1) * LANES]
                lse_ref[0, half, own_rows[t], :] = lse_acc[:, half * LANES:(half + 1) * LANES]

    if dilation * units <= ATTN_STATIC_BLOCKS:
        blocks = [(u, r) for u in range(units) for r in range(dilation)]
        for j in range(0, len(blocks), tasks):
            run(blocks[j:j + tasks])
    else:
        for u in range(units):
            def residues(j, carry, u=u):
                run([(u, j * tasks + t) for t in range(tasks)])
                return carry
            lax.fori_loop(0, dilation // tasks, residues, 0)


def _attention_prompt(att, slopes, group):
    b, _, s, _ = att.shape
    d = DILATIONS[group]
    span = A_BLOCK * d
    units = max(1, PROMPT_ROW_TILE // span)
    rows = span * units
    tasks = ATTN_TASKS * (2 if d > SUBLANES else 1)
    qb, kb, vb = group, N_GROUPS + group, 2 * N_GROUPS + group
    own = lambda col: pl.BlockSpec((1, HALVES, rows, LANES), lambda i, n: (i, col, n, 0))
    prev = lambda col: pl.BlockSpec((1, HALVES, span, LANES), lambda i, n: (i, col, jnp.maximum(n * units - 1, 0), 0))
    out_spec = pl.BlockSpec((1, HALVES, rows, LANES), lambda i, n: (i, 0, n, 0))
    return pl.pallas_call(
        functools.partial(_attn_prompt_kernel, group=group, dilation=d, units=units, tasks=tasks),
        out_shape=[jax.ShapeDtypeStruct((b, HALVES, s, LANES), F32)] * 2,
        grid=(b, s // rows),
        in_specs=[pl.BlockSpec(memory_space=pltpu.SMEM), own(qb), prev(kb), own(kb), prev(vb), own(vb)],
        out_specs=[out_spec, out_spec],
        compiler_params=_cparams(("arbitrary", "arbitrary")),
        name=f"attn_prompt_g{group}",
    )(slopes, att, att, att, att, att)


def _sample_bias(slopes, group, t_new):
    d = DILATIONS[group]
    lb = WINDOWS[group]
    t = np.arange(t_new)

    def table(key_pos, width):
        diff = (lb + t)[:, None] - key_pos[None, :]
        ok = (diff >= 0) & (diff % d == 0) & (diff // d <= STEPS_BACK)
        dist = jnp.asarray(np.where(ok, diff, 0), F32)
        bias = jnp.where(jnp.asarray(ok)[None], -slopes[group][:, None, None] * dist[None], -jnp.inf)
        bias = bias.reshape(HEADS_PER_GROUP * t_new, key_pos.shape[0])
        return jnp.pad(bias, ((0, 0), (0, width - key_pos.shape[0])), constant_values=-jnp.inf)

    return table(np.arange(lb), lb), table(lb + t, LANES)


def _attn_sample_kernel(q_ref, k_ref, v_ref, cache_ref, bias_ref, biasn_ref, o_ref, lse_ref, *, bb, per_iter, t_new):
    nq = HEADS_PER_GROUP * t_new
    row_head = lax.broadcasted_iota(jnp.int32, (nq, GROUP_W), 0) // t_new
    lane_head2 = lax.broadcasted_iota(jnp.int32, (nq, GROUP_W), 1) // HEAD_DIM_A
    lane_head = lax.broadcasted_iota(jnp.int32, (1, GROUP_W), 1) // HEAD_DIM_A
    pad_n = jnp.zeros((LANES - t_new, GROUP_W), F32)

    def new_rows(ref, i):
        return jnp.concatenate([ref[i, half] for half in range(HALVES)], axis=-1)

    def body(it, carry):
        seqs = [it * per_iter + s for s in range(per_iter)]
        each = lambda f: [f(i) for i in seqs]
        n = range(per_iter)
        q = each(lambda i: new_rows(q_ref, i) * (HEAD_DIM_A ** -0.5))
        qm = [jnp.where(row_head == lane_head2, jnp.concatenate([x] * HEADS_PER_GROUP, axis=0), 0.0).astype(BF16)
              for x in q]
        kt = each(lambda i: cache_ref[0, i, 0].astype(BF16))
        vt = each(lambda i: cache_ref[0, i, 1].astype(BF16))
        kn = each(lambda i: jnp.concatenate([new_rows(k_ref, i), pad_n], axis=0).astype(BF16))
        vn = each(lambda i: jnp.concatenate([new_rows(v_ref, i), pad_n], axis=0).astype(BF16))
        sc = [jnp.dot(qm[c], kt[c], preferred_element_type=F32) + bias_ref[...] for c in n]
        sn = [lax.dot_general(qm[c], kn[c], NT_DIMS, preferred_element_type=F32) + biasn_ref[...] for c in n]
        m = [jnp.maximum(sc[c].max(-1, keepdims=True), sn[c].max(-1, keepdims=True)) for c in n]
        pc = [jnp.exp(sc[c] - m[c]) for c in n]
        pn = [jnp.exp(sn[c] - m[c]) for c in n]
        den = [pc[c].sum(-1, keepdims=True) + pn[c].sum(-1, keepdims=True) for c in n]
        of = [lax.dot_general((pc[c] / den[c]).astype(BF16), vt[c], NT_DIMS, preferred_element_type=F32)
              + jnp.dot((pn[c] / den[c]).astype(BF16), vn[c], preferred_element_type=F32) for c in n]
        for c, i in enumerate(seqs):
            lse_col = m[c] + jnp.log(den[c])
            o = jnp.zeros((t_new, GROUP_W), F32)
            lse = jnp.zeros((t_new, GROUP_W), F32)
            for h in range(HEADS_PER_GROUP):
                head = lane_head == h
                o = jnp.where(head, of[c][h * t_new:(h + 1) * t_new, :], o)
                lse = jnp.where(head, lse_col[h * t_new:(h + 1) * t_new, :], lse)
            for half in range(HALVES):
                o_ref[i, half] = o[:, half * LANES:(half + 1) * LANES]
                lse_ref[i, half] = lse[:, half * LANES:(half + 1) * LANES]
        return carry

    lax.fori_loop(0, bb // per_iter, body, 0)


def _attention_sample(att, cache_t, layer, slopes, group, bb, per_iter):
    nseq, _, t_new, _ = att.shape
    lb = WINDOWS[group]
    bias, bias_new = _sample_bias(slopes, group, t_new)
    qb, kb, vb = group, N_GROUPS + group, 2 * N_GROUPS + group
    new_spec = lambda col: pl.BlockSpec((bb, HALVES, t_new, LANES), lambda i: (i, col, 0, 0))
    out_spec = pl.BlockSpec((bb, HALVES, t_new, LANES), lambda i: (i, 0, 0, 0))
    return pl.pallas_call(
        functools.partial(_attn_sample_kernel, bb=bb, per_iter=per_iter, t_new=t_new),
        out_shape=[jax.ShapeDtypeStruct((nseq, HALVES, t_new, LANES), F32)] * 2,
        grid=(nseq // bb,),
        in_specs=[new_spec(qb), new_spec(kb), new_spec(vb),
                  pl.BlockSpec((1, bb, 2, GROUP_W, lb), lambda i: (layer, i, 0, 0, 0)),
                  pl.BlockSpec(bias.shape, lambda i: (0, 0)), pl.BlockSpec(bias_new.shape, lambda i: (0, 0))],
        out_specs=[out_spec, out_spec],
        compiler_params=_cparams(("arbitrary",)),
        name=f"attn_sample_g{group}",
    )(att, att, att, cache_t, bias, bias_new)


def _unit_lower_inverses(lowers, ri, ci):
    c = lowers[0].shape[0]
    base = min(TRI_BASE, c)
    ident = (ri == ci).astype(F32)
    same = (ri // base) == (ci // base)
    powers = [jnp.where(same, l, 0.0) for l in lowers]
    invs = [ident - p for p in powers]
    for _ in range(int(np.log2(base)) - 1):
        powers = [_bdot(p, p) for p in powers]
        invs = [t + _bdot(t, p) for t, p in zip(invs, powers)]
    size = base
    while size < c:
        pair = ((ri // (2 * size)) == (ci // (2 * size))) & ((ri // size) % 2 == 1) & ((ci // size) % 2 == 0)
        offs = [_bdot(jnp.where(pair, l, 0.0), t) for l, t in zip(lowers, invs)]
        invs = [t - _bdot(t, o) for t, o in zip(invs, offs)]
        size *= 2
    return invs


def _gdn_kernel(*refs, bb, per_iter, chunk, layered, single_chunk, aliased):
    (q_ref, k_ref, v_ref, ab_ref, conv0_ref, s0_ref, cw_ref, par_ref), refs = refs[:8], refs[8 + int(aliased):]
    o_ref, s_ref, ext_ref = refs
    c_idx = pl.program_id(1)
    C = chunk
    first_row = SUBLANES - (CONV_WIDTH - 1)
    if layered:
        s0_ref, s_ref = s0_ref.at[0], s_ref.at[0]
    cw_ref, par_ref = cw_ref.at[0], par_ref.at[0]
    s_prev = s0_ref if single_chunk else s_ref

    @pl.when(c_idx == 0)
    def _():
        ext_ref[:, 0:SUBLANES, :] = conv0_ref[...]
        if not single_chunk:
            s_ref[...] = s0_ref[...]

    @pl.when(c_idx > 0)
    def _():
        ext_ref[:, 0:SUBLANES, :] = ext_ref[:, C:C + SUBLANES, :]

    ext_ref[:, SUBLANES:SUBLANES + C, 0:WIDTH_B] = q_ref[...].astype(F32)
    ext_ref[:, SUBLANES:SUBLANES + C, WIDTH_B:2 * WIDTH_B] = k_ref[...].astype(F32)
    ext_ref[:, SUBLANES:SUBLANES + C, 2 * WIDTH_B:3 * WIDTH_B] = v_ref[...].astype(F32)

    ri = lax.broadcasted_iota(jnp.int32, (C, C), 0)
    ci = lax.broadcasted_iota(jnp.int32, (C, C), 1)
    causal = ri >= ci
    strict = ri > ci
    tril = causal.astype(F32)
    neg_a = -jnp.exp(par_ref[0:1, :])
    dt_bias = par_ref[1:2, :]
    norm_w = par_ref[2:3, :]
    pad_rows = jnp.zeros((HEAD_DIM_B - C, HEAD_DIM_B), F32) if C < HEAD_DIM_B else None

    def pad128(x):
        return x if pad_rows is None else jnp.concatenate([x, pad_rows], axis=0)

    def conv(i, col):
        acc = ext_ref[i, pl.ds(first_row, C), pl.ds(col, HEAD_DIM_B)] * cw_ref[0:1, pl.ds(col, HEAD_DIM_B)]
        for j in range(1, CONV_WIDTH):
            acc = acc + (ext_ref[i, pl.ds(first_row + j, C), pl.ds(col, HEAD_DIM_B)]
                         * cw_ref[j:j + 1, pl.ds(col, HEAD_DIM_B)])
        return _silu_of_half(acc)

    def l2n(x, scale=1.0):
        return x * (lax.rsqrt(jnp.sum(x * x, axis=-1, keepdims=True) + RMS_EPS) * scale)

    def gates(i):
        ab = ab_ref[i, 0]
        z = ab + dt_bias
        softplus = jnp.maximum(z, 0.0) + jnp.log(1.0 + jnp.exp(-jnp.abs(z)))
        g_all = neg_a * softplus
        beta_all = _sigmoid(ab)
        gcum_all = jnp.dot(tril, g_all, preferred_element_type=F32, precision=HIGHEST)
        return beta_all, gcum_all, pad128(gcum_all).T

    def body(it, carry):
        seqs = [it * per_iter + s for s in range(per_iter)]
        beta_all, gcum_all, gcum_rows = zip(*[gates(i) for i in seqs])
        chains = [(s, h) for s in range(per_iter) for h in range(N_HEADS_B)]
        each = lambda f: [f(s, h) for s, h in chains]
        n = range(len(chains))
        states = each(lambda s, h: s_prev[seqs[s], h])
        q = each(lambda s, h: l2n(conv(seqs[s], h * HEAD_DIM_B), HEAD_DIM_B ** -0.5))
        k = each(lambda s, h: l2n(conv(seqs[s], WIDTH_B + h * HEAD_DIM_B)))
        v = each(lambda s, h: conv(seqs[s], 2 * WIDTH_B + h * HEAD_DIM_B))
        beta = each(lambda s, h: beta_all[s][:, N_HEADS_B + h:N_HEADS_B + h + 1])
        gcum = each(lambda s, h: gcum_all[s][:, h:h + 1])
        grow = each(lambda s, h: gcum_rows[s][h:h + 1, 0:C])
        decay = [jnp.exp(jnp.where(causal, gcum[c] - grow[c], -jnp.inf)) for c in n]
        qk = [_bdot_nt(jnp.concatenate([q[c], k[c]], axis=0), k[c]) for c in n]
        intra = [qk[c][0:C] * decay[c] for c in n]
        lower = [jnp.where(strict, qk[c][C:2 * C] * beta[c] * decay[c], 0.0) for c in n]
        inv = _unit_lower_inverses(lower, ri, ci)
        e_g = [jnp.exp(gcum[c]) for c in n]
        uw = [_bdot(inv[c], jnp.concatenate([v[c] * beta[c], k[c] * (beta[c] * e_g[c])], axis=-1)) for c in n]
        ws_qs = [_bdot(jnp.concatenate([uw[c][:, HEAD_DIM_B:2 * HEAD_DIM_B], q[c] * e_g[c]], axis=0), states[c])
                 for c in n]
        v_new = [uw[c][:, 0:HEAD_DIM_B] - ws_qs[c][0:C] for c in n]
        g_last = [gcum[c][C - 1:C, :] for c in n]
        k_dec = [k[c] * jnp.exp(g_last[c] - gcum[c]) for c in n]
        o = [ws_qs[c][C:2 * C] + _bdot(intra[c], v_new[c]) for c in n]
        new_states = [states[c] * jnp.exp(g_last[c]) + _bdot(pad128(k_dec[c]).T, pad128(v_new[c])) for c in n]
        for c, (s, h) in enumerate(chains):
            s_ref[seqs[s], h] = new_states[c]
            o_c = o[c] * lax.rsqrt(jnp.mean(o[c] * o[c], axis=-1, keepdims=True) + RMS_EPS) * norm_w
            o_ref[seqs[s], :, pl.ds(h * HEAD_DIM_B, HEAD_DIM_B)] = o_c
        return carry

    lax.fori_loop(0, bb // per_iter, body, 0)


def _gated_deltanet(rest, att, conv0, s0, layer, conv_w, params, bb, per_iter, chunk, state_buf=None):
    nseq, t, _ = rest.shape
    qb = REST_QKVB // WIDTH_B
    col_spec = lambda col: pl.BlockSpec((bb, chunk, WIDTH_B), lambda i, c: (i, c, col))
    state_shape = (bb, N_HEADS_B, HEAD_DIM_B, HEAD_DIM_B)
    layered = s0.ndim == 5
    if layered:
        state_spec = pl.BlockSpec((1,) + state_shape, lambda i, c: (layer, i, 0, 0, 0))
        state_out = jax.ShapeDtypeStruct(s0.shape, F32)
    else:
        state_spec = pl.BlockSpec(state_shape, lambda i, c: (i, 0, 0, 0))
        state_out = jax.ShapeDtypeStruct((nseq,) + state_shape[1:], F32)
    in_specs = [col_spec(qb), col_spec(qb + 1), col_spec(qb + 2),
                pl.BlockSpec((bb, 1, chunk, LANES), lambda i, c: (i, AB_BLOCK, c, 0)),
                pl.BlockSpec((bb, SUBLANES, CONV_DIM), lambda i, c: (i, 0, 0)),
                state_spec,
                pl.BlockSpec((1, SUBLANES, CONV_DIM), lambda i, c: (layer, 0, 0)),
                pl.BlockSpec((1, SUBLANES, LANES), lambda i, c: (layer, 0, 0))]
    args = [rest, rest, rest, att, conv0, s0, conv_w, params]
    aliases = {}
    if state_buf is not None:
        aliases = {len(args): 1}
        in_specs.append(pl.BlockSpec(memory_space=pl.ANY))
        args.append(state_buf)
    return pl.pallas_call(
        functools.partial(_gdn_kernel, bb=bb, per_iter=per_iter, chunk=chunk, layered=layered,
                          single_chunk=(t == chunk), aliased=state_buf is not None),
        out_shape=[jax.ShapeDtypeStruct((nseq, t, WIDTH_B), F32), state_out],
        grid=(nseq // bb, t // chunk),
        in_specs=in_specs,
        out_specs=[pl.BlockSpec((bb, chunk, WIDTH_B), lambda i, c: (i, c, 0)), state_spec],
        scratch_shapes=[pltpu.VMEM((bb, chunk + SUBLANES, CONV_DIM), F32)],
        input_output_aliases=aliases,
        compiler_params=_cparams(("arbitrary", "arbitrary")),
        name="gated_deltanet",
    )(*args)


def _out_kernel(x_ref, mod_ref, o1_ref, o2_ref, o3_ref, l1_ref, l2_ref, l3_ref, za_ref, ob_ref, zb_ref, ga_ref, gb_ref,
                wa_ref, wb_ref, wo_ref, lng_ref, lnb_ref, y_ref):
    bs, bt, _ = x_ref.shape
    rows = bs * bt
    flat = lambda ref: ref[...].reshape(rows, ref.shape[-1]).astype(F32)
    group = lambda ref: jnp.concatenate([ref[:, half].reshape(rows, LANES) for half in range(HALVES)], axis=-1)
    l1, l2, l3 = group(l1_ref), group(l2_ref), group(l3_ref)
    m = jnp.maximum(jnp.maximum(l1, l2), l3)
    e1, e2, e3 = jnp.exp(l1 - m), jnp.exp(l2 - m), jnp.exp(l3 - m)
    inv_tot = 1.0 / (e1 + e2 + e3)
    att = jnp.concatenate([group(o1_ref) * (e1 * inv_tot), group(o2_ref) * (e2 * inv_tot),
                           group(o3_ref) * (e3 * inv_tot)], axis=-1)
    half_a = jnp.dot((att * _silu_of_half(flat(za_ref))).astype(BF16), wa_ref[0], preferred_element_type=F32)
    half_b = jnp.dot((flat(ob_ref) * _silu_of_half(flat(zb_ref))).astype(BF16), wb_ref[0], preferred_element_type=F32)
    merged = (half_a + half_a * jnp.tanh(flat(ga_ref))) + (half_b + half_b * jnp.tanh(flat(gb_ref)))
    out = jnp.dot(merged.astype(BF16), wo_ref[0], preferred_element_type=F32).reshape(bs, bt, D_MODEL)
    gate = mod_ref[:, :, 2 * D_MODEL:3 * D_MODEL]
    y = DEEPNORM_ALPHA * x_ref[...] + gate * out
    mu = jnp.mean(y, axis=-1, keepdims=True)
    yc = y - mu
    var = jnp.mean(yc * yc, axis=-1, keepdims=True)
    y_ref[...] = yc * lax.rsqrt(var + LN_EPS) * lng_ref[...] + lnb_ref[...]


def _output_block(x, mod, o_g, lse_g, o_b, rest, wa, wb, wo, ln_g, ln_b, layer, bs, bt):
    nseq, t, _ = x.shape
    row = lambda w, col: pl.BlockSpec((bs, bt, w), lambda i, k: (i, k, col))
    halves = pl.BlockSpec((bs, HALVES, bt, LANES), lambda i, k: (i, 0, k, 0))
    per_layer = lambda shape: pl.BlockSpec((1,) + shape, lambda i, k: (layer,) + (0,) * len(shape))
    return pl.pallas_call(
        _out_kernel,
        out_shape=jax.ShapeDtypeStruct((nseq, t, D_MODEL), F32),
        grid=(nseq // bs, t // bt),
        in_specs=[row(D_MODEL, 0),
                  pl.BlockSpec((bs, 1, 3 * D_MODEL), lambda i, k: (i, 0, 0)),
                  halves, halves, halves, halves, halves, halves,
                  row(WIDTH_A, REST_ZA // WIDTH_A),
                  row(WIDTH_B, 0),
                  row(WIDTH_B, REST_ZB // WIDTH_B),
                  row(D_MODEL, REST_GA // D_MODEL),
                  row(D_MODEL, REST_GB // D_MODEL),
                  per_layer((WIDTH_A, D_MODEL)), per_layer((WIDTH_B, D_MODEL)), per_layer((D_MODEL, D_MODEL)),
                  per_layer((1, D_MODEL)), per_layer((1, D_MODEL))],
        out_specs=row(D_MODEL, 0),
        compiler_params=_cparams(("arbitrary", "arbitrary")),
        name="output_block",
    )(x, mod, o_g[0], o_g[1], o_g[2], lse_g[0], lse_g[1], lse_g[2], rest, o_b, rest, rest, rest,
      wa, wb, wo, ln_g, ln_b)


def _alibi_slopes():
    i = jnp.arange(1, N_HEADS_A + 1, dtype=F32)
    return jnp.exp2(-8.0 * i / N_HEADS_A).reshape(N_GROUPS, HEADS_PER_GROUP)


def _split_w_in(w_in):
    zeros = lambda n: jnp.zeros(w_in.shape[:2] + (n,), BF16)
    att = jnp.concatenate([w_in[:, :, :SRC_ZA].astype(BF16), w_in[:, :, SRC_AB:SRC_GA].astype(BF16),
                           zeros(LANES - 2 * N_HEADS_B)], axis=-1)
    half = lambda lo, hi: (0.5 * w_in[:, :, lo:hi]).astype(BF16)
    rest = jnp.concatenate([half(SRC_ZA, SRC_QKVB), w_in[:, :, SRC_QKVB:SRC_ZB].astype(BF16), half(SRC_ZB, SRC_AB),
                            zeros(REST_GA - (SRC_AB - SRC_ZA)), half(SRC_GA, w_in.shape[-1])], axis=-1)
    return att, rest


def _split_kv(att, group, start):
    b = att.shape[0]
    k = att[:, (N_GROUPS + group) * HALVES:(N_GROUPS + group + 1) * HALVES, start:, :]
    v = att[:, (2 * N_GROUPS + group) * HALVES:(2 * N_GROUPS + group + 1) * HALVES, start:, :]
    kv = jnp.stack([k, v], axis=1)
    kv = kv.transpose(0, 3, 1, 2, 4)
    return kv.reshape(b, kv.shape[1], 2, HEADS_PER_GROUP, HEAD_DIM_A)


def _feature_major(cache):
    d, b, l = cache.shape[:3]
    return cache.transpose(0, 1, 3, 4, 5, 2).reshape(d, b, 2, GROUP_W, l)


def kernel(x_prompt, x_sample, c_prompt, c_sample, cache_win1, cache_win2, cache_win3, state_gdn, state_conv,
           w_ada, b_ada, w_in, conv_w, a_log, dt_bias, gdn_norm, w_out_a, w_out_b, w_o, ln_g, ln_b):
    n_p, seq, _ = x_prompt.shape
    n_s, t_new, _ = x_sample.shape
    caches = tuple(_feature_major(c) for c in (cache_win1, cache_win2, cache_win3))
    sample_attn_tiling = ((8, 4), (8, 2), (2, 2))
    slopes = _alibi_slopes()

    w_att, w_rest = _split_w_in(w_in)
    wa = (0.5 * w_out_a).astype(BF16)
    wb = (0.5 * w_out_b).astype(BF16)
    wo = w_o.astype(BF16)
    ln_g3 = ln_g.reshape(DEPTH, 1, D_MODEL)
    ln_b3 = ln_b.reshape(DEPTH, 1, D_MODEL)
    conv_w_p = jnp.pad(0.5 * conv_w, ((0, 0), (0, SUBLANES - CONV_WIDTH), (0, 0)))
    params = jnp.zeros((DEPTH, SUBLANES, LANES), F32)
    params = params.at[:, 0, :N_HEADS_B].set(a_log).at[:, 1, :N_HEADS_B].set(dt_bias).at[:, 2, :].set(gdn_norm)

    n_all = n_p + n_s
    n_pad = -(-n_all // SUBLANES) * SUBLANES
    c_all = jnp.pad(jnp.concatenate([c_prompt, c_sample], axis=0), ((0, n_pad - n_all), (0, 0)))
    mod_all = _modulation(c_all, w_ada, b_ada)

    zero_conv = jnp.zeros((n_p, SUBLANES, CONV_DIM), F32)
    zero_state = jnp.zeros((n_p, N_HEADS_B, HEAD_DIM_B, HEAD_DIM_B), F32)
    conv_s = jnp.pad(state_conv, ((0, 0), (0, 0), (SUBLANES - (CONV_WIDTH - 1), 0), (0, 0)))

    xp, xs = x_prompt, x_sample
    acc_p = [[] for _ in range(N_GROUPS + 2)]
    acc_s = [[] for _ in range(N_GROUPS + 2)]
    gdn_sample = None
    for l in range(DEPTH):
        mod_p = mod_all[l, :n_p, None, :]
        mod_s = mod_all[l, n_p:n_all, None, :]

        att_p, rest_p = _in_projection(xp, mod_p, w_att, w_rest, l, 1, PROMPT_ROW_TILE, BF16)
        att_s, rest_s = _in_projection(xs, mod_s, w_att, w_rest, l, SAMPLE_SEQ_TILE, t_new, F32)

        o_p, lse_p, o_s, lse_s = [], [], [], []
        for g in range(N_GROUPS):
            o, lse = _attention_prompt(att_p, slopes, g)
            o_p.append(o)
            lse_p.append(lse)
            o, lse = _attention_sample(att_s, caches[g], l, slopes, g, *sample_attn_tiling[g])
            o_s.append(o)
            lse_s.append(lse)

        ob_p, st_p = _gated_deltanet(rest_p, att_p, zero_conv, zero_state, l, conv_w_p, params, PROMPT_GDN_SEQS,
                                     PROMPT_GDN_SEQS, PROMPT_GDN_CHUNK)
        ob_s, gdn_sample = _gated_deltanet(rest_s, att_s, conv_s[l], state_gdn, l, conv_w_p, params,
                                           *SAMPLE_GDN_TILING, t_new, state_buf=gdn_sample)

        xp_new = _output_block(xp, mod_p, o_p, lse_p, ob_p, rest_p, wa, wb, wo, ln_g3, ln_b3, l,
                               1, PROMPT_OUT_ROW_TILE)
        xs_new = _output_block(xs, mod_s, o_s, lse_s, ob_s, rest_s, wa, wb, wo, ln_g3, ln_b3, l,
                               SAMPLE_SEQ_TILE, t_new)

        for g in range(N_GROUPS):
            acc_p[g].append(_split_kv(att_p, g, seq - min(WINDOWS[g], seq)))
            acc_s[g].append(_split_kv(att_s, g, 0))
        acc_p[N_GROUPS].append(st_p)
        acc_p[N_GROUPS + 1].append(rest_p[:, seq - (CONV_WIDTH - 1):, REST_QKVB:REST_QKVB + CONV_DIM].astype(F32))
        acc_s[N_GROUPS + 1].append(rest_s[:, t_new - (CONV_WIDTH - 1):, REST_QKVB:REST_QKVB + CONV_DIM])
        xp, xs = xp_new, xs_new

    acc_s[N_GROUPS] = None
    outs_p = [jnp.stack(a) for a in acc_p]
    outs_s = [gdn_sample if a is None else jnp.stack(a) for a in acc_s]
    return (xp, xs, *outs_p, *outs_s)
```

```python
import functools

import numpy as np
import jax
import jax.numpy as jnp
from jax import lax
from jax.experimental import pallas as pl
from jax.experimental.pallas import tpu as pltpu

F32 = jnp.float32
BF16 = jnp.bfloat16
HIGHEST = lax.Precision.HIGHEST

D_MODEL = 1024
DEPTH = 4
DILATIONS = (1, 4, 16)
WINDOWS = (128, 512, 2048)
N_GROUPS = 3
HEADS_PER_GROUP = 4
HEAD_DIM_A = 64
GROUP_W = HEADS_PER_GROUP * HEAD_DIM_A
WIDTH_A = N_GROUPS * GROUP_W
N_HEADS_A = N_GROUPS * HEADS_PER_GROUP
STEPS_BACK = 128
A_BLOCK = 128
N_HEADS_B = 6
HEAD_DIM_B = 128
WIDTH_B = N_HEADS_B * HEAD_DIM_B
CONV_WIDTH = 4
CONV_DIM = 3 * WIDTH_B
DEEPNORM_ALPHA = (2 * DEPTH) ** 0.25
LN_EPS = 1e-5
RMS_EPS = 1e-6
LANES = 128
SUBLANES = 8
HALVES = GROUP_W // LANES

SRC_ZA = 3 * WIDTH_A
SRC_QKVB = SRC_ZA + WIDTH_A
SRC_ZB = SRC_QKVB + CONV_DIM
SRC_AB = SRC_ZB + WIDTH_B
SRC_GA = SRC_AB + 2 * N_HEADS_B
SRC_GB = SRC_GA + D_MODEL

AB_BLOCK = 3 * WIDTH_A // LANES
ATT_BLOCKS = AB_BLOCK + 1
ATT_W = ATT_BLOCKS * LANES
REST_ZA = 0
REST_QKVB = 768
REST_ZB = 3072
REST_GA = 4096
REST_GB = 5120
REST_W = 6144
REST_TILE = 2048

PROMPT_GDN_CHUNK = 128
PROMPT_GDN_SEQS = 2
TRI_BASE = 16
ATTN_TASKS = 2
ATTN_STATIC_BLOCKS = 4
PROMPT_ROW_TILE = 512
PROMPT_OUT_ROW_TILE = 256
SAMPLE_SEQ_TILE = 32
SAMPLE_GDN_TILING = (8, 4)
VMEM_LIMIT = 48 * 1024 * 1024

NT_DIMS = (((1,), (1,)), ((), ()))


def _cparams(sem):
    return pltpu.CompilerParams(dimension_semantics=sem, vmem_limit_bytes=VMEM_LIMIT)


def _sigmoid(x):
    return 0.5 * jnp.tanh(0.5 * x) + 0.5


def _silu_of_half(h):
    return h + h * jnp.tanh(h)


def _silu(x):
    return _silu_of_half(0.5 * x)


def _bdot(a, b):
    return jnp.dot(a.astype(BF16), b.astype(BF16), preferred_element_type=F32)


def _bdot_nt(a, b):
    return lax.dot_general(a.astype(BF16), b.astype(BF16), NT_DIMS, preferred_element_type=F32)


def _mod_kernel(c_ref, w_ref, b_ref, o_ref):
    o_ref[0] = _bdot(_silu(c_ref[...]), w_ref[0]) + b_ref[0]


def _modulation(c_all, w_ada, b_ada):
    n = c_all.shape[0]
    return pl.pallas_call(
        _mod_kernel,
        out_shape=jax.ShapeDtypeStruct((DEPTH, n, 3 * D_MODEL), F32),
        grid=(DEPTH, 3),
        in_specs=[
            pl.BlockSpec((n, D_MODEL), lambda l, j: (0, 0)),
            pl.BlockSpec((1, D_MODEL, D_MODEL), lambda l, j: (l, 0, j)),
            pl.BlockSpec((1, 1, D_MODEL), lambda l, j: (l, 0, j)),
        ],
        out_specs=pl.BlockSpec((1, n, D_MODEL), lambda l, j: (l, 0, j)),
        compiler_params=_cparams(("arbitrary", "arbitrary")),
        name="adaln_modulation",
    )(c_all, w_ada, b_ada.reshape(DEPTH, 1, 3 * D_MODEL))


def _modulated(x_ref, mod_ref):
    bs, bt, _ = x_ref.shape
    shift = mod_ref[:, :, 0:D_MODEL]
    scale = mod_ref[:, :, D_MODEL:2 * D_MODEL]
    h = x_ref[...] * (1.0 + scale) + shift
    return h.reshape(bs * bt, D_MODEL).astype(BF16)


def _inproj_att_kernel(x_ref, mod_ref, w_ref, o_ref):
    bs, bt, _ = x_ref.shape
    res = jnp.dot(_modulated(x_ref, mod_ref), w_ref[0], preferred_element_type=F32)
    for j in range(ATT_BLOCKS):
        o_ref[:, j] = res[:, j * LANES:(j + 1) * LANES].reshape(bs, bt, LANES)


def _inproj_rest_kernel(x_ref, mod_ref, w_ref, scale_ref, o_ref):
    bs, bt, _ = x_ref.shape
    res = jnp.dot(_modulated(x_ref, mod_ref), w_ref[0], preferred_element_type=F32)
    res = res * scale_ref[...]
    o_ref[...] = res.reshape(bs, bt, REST_TILE).astype(o_ref.dtype)


def _rest_column_scale():
    scale = np.full((1, REST_W), 0.5, np.float32)
    scale[:, REST_QKVB:REST_ZB] = 1.0
    return jnp.asarray(scale)


def _in_projection(x, mod, w_att, w_rest, layer, bs, bt, rest_dtype):
    nseq, t, _ = x.shape
    att = pl.pallas_call(
        _inproj_att_kernel,
        out_shape=jax.ShapeDtypeStruct((nseq, ATT_BLOCKS, t, LANES), F32),
        grid=(nseq // bs, t // bt),
        in_specs=[
            pl.BlockSpec((bs, bt, D_MODEL), lambda i, k: (i, k, 0)),
            pl.BlockSpec((bs, 1, 3 * D_MODEL), lambda i, k: (i, 0, 0)),
            pl.BlockSpec((1, D_MODEL, ATT_W), lambda i, k: (layer, 0, 0)),
        ],
        out_specs=pl.BlockSpec((bs, ATT_BLOCKS, bt, LANES), lambda i, k: (i, 0, k, 0)),
        compiler_params=_cparams(("arbitrary", "arbitrary")),
        name="in_projection_att",
    )(x, mod, w_att)
    rest = pl.pallas_call(
        _inproj_rest_kernel,
        out_shape=jax.ShapeDtypeStruct((nseq, t, REST_W), rest_dtype),
        grid=(REST_W // REST_TILE, nseq // bs, t // bt),
        in_specs=[
            pl.BlockSpec((bs, bt, D_MODEL), lambda j, i, k: (i, k, 0)),
            pl.BlockSpec((bs, 1, 3 * D_MODEL), lambda j, i, k: (i, 0, 0)),
            pl.BlockSpec((1, D_MODEL, REST_TILE), lambda j, i, k: (layer, 0, j)),
            pl.BlockSpec((1, REST_TILE), lambda j, i, k: (0, j)),
        ],
        out_specs=pl.BlockSpec((bs, bt, REST_TILE), lambda j, i, k: (i, k, j)),
        compiler_params=_cparams(("arbitrary", "arbitrary", "arbitrary")),
        name="in_projection_rest",
    )(x, mod, w_rest, _rest_column_scale())
    return att, rest


def _attn_prompt_kernel(slopes_ref, q_ref, kp_ref, ko_ref, vp_ref, vo_ref, o_ref, lse_ref, *, group, dilation, units,
                        tasks):
    n = pl.program_id(1)
    span = A_BLOCK * dilation
    qi = lax.broadcasted_iota(jnp.int32, (A_BLOCK, 2 * A_BLOCK), 0)
    kj = lax.broadcasted_iota(jnp.int32, (A_BLOCK, 2 * A_BLOCK), 1)
    steps = A_BLOCK + qi - kj
    band = (steps >= 0) & (steps <= STEPS_BACK)
    dist = (steps * dilation).astype(F32)
    alibi = [jnp.where(band, -slopes_ref[group, h] * dist, -jnp.inf) for h in range(HEADS_PER_GROUP)]
    no_prev = jnp.where((kj >= A_BLOCK) | (n > 0), 0.0, -jnp.inf)
    lane_head = lax.broadcasted_iota(jnp.int32, (1, GROUP_W), 1) // HEAD_DIM_A
    heads = range(HEADS_PER_GROUP)

    def rows_of(ref, rows):
        return jnp.concatenate([ref[0, half, rows, :] for half in range(HALVES)], axis=-1)

    def run(tasks):
        q, k, v, own_rows = [], [], [], []
        for u, r in tasks:
            own = pl.ds(u * span + r, A_BLOCK, stride=dilation)
            if u == 0:
                prev_k, prev_v, prev = kp_ref, vp_ref, pl.ds(r, A_BLOCK, stride=dilation)
            else:
                prev_k, prev_v, prev = ko_ref, vo_ref, pl.ds((u - 1) * span + r, A_BLOCK, stride=dilation)
            own_rows.append(own)
            q.append(rows_of(q_ref, own) * (HEAD_DIM_A ** -0.5))
            k.append(jnp.concatenate([rows_of(prev_k, prev), rows_of(ko_ref, own)], axis=0).astype(BF16))
            v.append(jnp.concatenate([rows_of(prev_v, prev), rows_of(vo_ref, own)], axis=0).astype(BF16))
        chains = [(t, h) for t in range(len(tasks)) for h in heads]
        s = [lax.dot_general(jnp.where(lane_head == h, q[t], 0.0).astype(BF16), k[t], NT_DIMS,
                             preferred_element_type=F32) for t, h in chains]
        s = [s[c] + alibi[h] for c, (t, h) in enumerate(chains)]
        s = [s[c] + no_prev if tasks[t][0] == 0 else s[c] for c, (t, h) in enumerate(chains)]
        m = [x.max(-1, keepdims=True) for x in s]
        p = [jnp.exp(x - mx) for x, mx in zip(s, m)]
        den = [x.sum(-1, keepdims=True) for x in p]
        oh = [jnp.dot((p[c] / den[c]).astype(BF16), v[t], preferred_element_type=F32) for c, (t, h) in enumerate(chains)]
        for t in range(len(tasks)):
            o_acc = jnp.zeros((A_BLOCK, GROUP_W), F32)
            lse_acc = jnp.zeros((A_BLOCK, GROUP_W), F32)
            for h in heads:
                c = t * HEADS_PER_GROUP + h
                o_acc = jnp.where(lane_head == h, oh[c], o_acc)
                lse_acc = jnp.where(lane_head == h, m[c] + jnp.log(den[c]), lse_acc)
            for half in range(HALVES):
                o_ref[0, half, own_rows[t], :] = o_acc[:, half * LANES:(half + 1) * LANES]
                lse_ref[0, half, own_rows[t], :] = lse_acc[:, half * LANES:(half + 1) * LANES]

    if dilation * units <= ATTN_STATIC_BLOCKS:
        blocks = [(u, r) for u in range(units) for r in range(dilation)]
        for j in range(0, len(blocks), tasks):
            run(blocks[j:j + tasks])
    else:
        for u in range(units):
            def residues(j, carry, u=u):
                run([(u, j * tasks + t) for t in range(tasks)])
                return carry
            lax.fori_loop(0, dilation // tasks, residues, 0)


def _attention_prompt(att, slopes, group):
    b, _, s, _ = att.shape
    d = DILATIONS[group]
    span = A_BLOCK * d
    units = max(1, PROMPT_ROW_TILE // span)
    rows = span * units
    tasks = ATTN_TASKS * (2 if d > SUBLANES else 1)
    qb, kb, vb = group, N_GROUPS + group, 2 * N_GROUPS + group
    own = lambda col: pl.BlockSpec((1, HALVES, rows, LANES), lambda i, n: (i, col, n, 0))
    prev = lambda col: pl.BlockSpec((1, HALVES, span, LANES), lambda i, n: (i, col, jnp.maximum(n * units - 1, 0), 0))
    out_spec = pl.BlockSpec((1, HALVES, rows, LANES), lambda i, n: (i, 0, n, 0))
    return pl.pallas_call(
        functools.partial(_attn_prompt_kernel, group=group, dilation=d, units=units, tasks=tasks),
        out_shape=[jax.ShapeDtypeStruct((b, HALVES, s, LANES), F32)] * 2,
        grid=(b, s // rows),
        in_specs=[pl.BlockSpec(memory_space=pltpu.SMEM), own(qb), prev(kb), own(kb), prev(vb), own(vb)],
        out_specs=[out_spec, out_spec],
        compiler_params=_cparams(("arbitrary", "arbitrary")),
        name=f"attn_prompt_g{group}",
    )(slopes, att, att, att, att, att)


def _sample_bias(slopes, group, t_new):
    d = DILATIONS[group]
    lb = WINDOWS[group]
    t = np.arange(t_new)

    def table(key_pos, width):
        diff = (lb + t)[:, None] - key_pos[None, :]
        ok = (diff >= 0) & (diff % d == 0) & (diff // d <= STEPS_BACK)
        dist = jnp.asarray(np.where(ok, diff, 0), F32)
        bias = jnp.where(jnp.asarray(ok)[None], -slopes[group][:, None, None] * dist[None], -jnp.inf)
        bias = bias.reshape(HEADS_PER_GROUP * t_new, key_pos.shape[0])
        return jnp.pad(bias, ((0, 0), (0, width - key_pos.shape[0])), constant_values=-jnp.inf)

    return table(np.arange(lb), lb), table(lb + t, LANES)


def _attn_sample_kernel(q_ref, k_ref, v_ref, cache_ref, bias_ref, biasn_ref, o_ref, lse_ref, *, bb, per_iter, t_new):
    nq = HEADS_PER_GROUP * t_new
    row_head = lax.broadcasted_iota(jnp.int32, (nq, GROUP_W), 0) // t_new
    lane_head2 = lax.broadcasted_iota(jnp.int32, (nq, GROUP_W), 1) // HEAD_DIM_A
    lane_head = lax.broadcasted_iota(jnp.int32, (1, GROUP_W), 1) // HEAD_DIM_A
    pad_n = jnp.zeros((LANES - t_new, GROUP_W), F32)

    def new_rows(ref, i):
        return jnp.concatenate([ref[i, half] for half in range(HALVES)], axis=-1)

    def body(it, carry):
        seqs = [it * per_iter + s for s in range(per_iter)]
        each = lambda f: [f(i) for i in seqs]
        n = range(per_iter)
        q = each(lambda i: new_rows(q_ref, i) * (HEAD_DIM_A ** -0.5))
        qm = [jnp.where(row_head == lane_head2, jnp.concatenate([x] * HEADS_PER_GROUP, axis=0), 0.0).astype(BF16)
              for x in q]
        kt = each(lambda i: cache_ref[0, i, 0].astype(BF16))
        vt = each(lambda i: cache_ref[0, i, 1].astype(BF16))
        kn = each(lambda i: jnp.concatenate([new_rows(k_ref, i), pad_n], axis=0).astype(BF16))
        vn = each(lambda i: jnp.concatenate([new_rows(v_ref, i), pad_n], axis=0).astype(BF16))
        sc = [jnp.dot(qm[c], kt[c], preferred_element_type=F32) + bias_ref[...] for c in n]
        sn = [lax.dot_general(qm[c], kn[c], NT_DIMS, preferred_element_type=F32) + biasn_ref[...] for c in n]
        m = [jnp.maximum(sc[c].max(-1, keepdims=True), sn[c].max(-1, keepdims=True)) for c in n]
        pc = [jnp.exp(sc[c] - m[c]) for c in n]
        pn = [jnp.exp(sn[c] - m[c]) for c in n]
        den = [pc[c].sum(-1, keepdims=True) + pn[c].sum(-1, keepdims=True) for c in n]
        of = [lax.dot_general((pc[c] / den[c]).astype(BF16), vt[c], NT_DIMS, preferred_element_type=F32)
              + jnp.dot((pn[c] / den[c]).astype(BF16), vn[c], preferred_element_type=F32) for c in n]
        for c, i in enumerate(seqs):
            lse_col = m[c] + jnp.log(den[c])
            o = jnp.zeros((t_new, GROUP_W), F32)
            lse = jnp.zeros((t_new, GROUP_W), F32)
            for h in range(HEADS_PER_GROUP):
                head = lane_head == h
                o = jnp.where(head, of[c][h * t_new:(h + 1) * t_new, :], o)
                lse = jnp.where(head, lse_col[h * t_new:(h + 1) * t_new, :], lse)
            for half in range(HALVES):
                o_ref[i, half] = o[:, half * LANES:(half + 1) * LANES]
                lse_ref[i, half] = lse[:, half * LANES:(half + 1) * LANES]
        return carry

    lax.fori_loop(0, bb // per_iter, body, 0)


def _attention_sample(att, cache_t, layer, slopes, group, bb, per_iter):
    nseq, _, t_new, _ = att.shape
    lb = WINDOWS[group]
    bias, bias_new = _sample_bias(slopes, group, t_new)
    qb, kb, vb = group, N_GROUPS + group, 2 * N_GROUPS + group
    new_spec = lambda col: pl.BlockSpec((bb, HALVES, t_new, LANES), lambda i: (i, col, 0, 0))
    out_spec = pl.BlockSpec((bb, HALVES, t_new, LANES), lambda i: (i, 0, 0, 0))
    return pl.pallas_call(
        functools.partial(_attn_sample_kernel, bb=bb, per_iter=per_iter, t_new=t_new),
        out_shape=[jax.ShapeDtypeStruct((nseq, HALVES, t_new, LANES), F32)] * 2,
        grid=(nseq // bb,),
        in_specs=[new_spec(qb), new_spec(kb), new_spec(vb),
                  pl.BlockSpec((1, bb, 2, GROUP_W, lb), lambda i: (layer, i, 0, 0, 0)),
                  pl.BlockSpec(bias.shape, lambda i: (0, 0)), pl.BlockSpec(bias_new.shape, lambda i: (0, 0))],
        out_specs=[out_spec, out_spec],
        compiler_params=_cparams(("arbitrary",)),
        name=f"attn_sample_g{group}",
    )(att, att, att, cache_t, bias, bias_new)


def _unit_lower_inverses(lowers, ri, ci):
    c = lowers[0].shape[0]
    base = min(TRI_BASE, c)
    ident = (ri == ci).astype(F32)
    same = (ri // base) == (ci // base)
    powers = [jnp.where(same, l, 0.0) for l in lowers]
    invs = [ident - p for p in powers]
    for _ in range(int(np.log2(base)) - 1):
        powers = [_bdot(p, p) for p in powers]
        invs = [t + _bdot(t, p) for t, p in zip(invs, powers)]
    size = base
    while size < c:
        pair = ((ri // (2 * size)) == (ci // (2 * size))) & ((ri // size) % 2 == 1) & ((ci // size) % 2 == 0)
        offs = [_bdot(jnp.where(pair, l, 0.0), t) for l, t in zip(lowers, invs)]
        invs = [t - _bdot(t, o) for t, o in zip(invs, offs)]
        size *= 2
    return invs


def _gdn_kernel(*refs, bb, per_iter, chunk, layered, single_chunk, aliased):
    (q_ref, k_ref, v_ref, ab_ref, conv0_ref, s0_ref, cw_ref, par_ref), refs = refs[:8], refs[8 + int(aliased):]
    o_ref, s_ref, ext_ref = refs
    c_idx = pl.program_id(1)
    C = chunk
    first_row = SUBLANES - (CONV_WIDTH - 1)
    if layered:
        s0_ref, s_ref = s0_ref.at[0], s_ref.at[0]
    cw_ref, par_ref = cw_ref.at[0], par_ref.at[0]
    s_prev = s0_ref if single_chunk else s_ref

    @pl.when(c_idx == 0)
    def _():
        ext_ref[:, 0:SUBLANES, :] = conv0_ref[...]
        if not single_chunk:
            s_ref[...] = s0_ref[...]

    @pl.when(c_idx > 0)
    def _():
        ext_ref[:, 0:SUBLANES, :] = ext_ref[:, C:C + SUBLANES, :]

    ext_ref[:, SUBLANES:SUBLANES + C, 0:WIDTH_B] = q_ref[...].astype(F32)
    ext_ref[:, SUBLANES:SUBLANES + C, WIDTH_B:2 * WIDTH_B] = k_ref[...].astype(F32)
    ext_ref[:, SUBLANES:SUBLANES + C, 2 * WIDTH_B:3 * WIDTH_B] = v_ref[...].astype(F32)

    ri = lax.broadcasted_iota(jnp.int32, (C, C), 0)
    ci = lax.broadcasted_iota(jnp.int32, (C, C), 1)
    causal = ri >= ci
    strict = ri > ci
    tril = causal.astype(F32)
    neg_a = -jnp.exp(par_ref[0:1, :])
    dt_bias = par_ref[1:2, :]
    norm_w = par_ref[2:3, :]
    pad_rows = jnp.zeros((HEAD_DIM_B - C, HEAD_DIM_B), F32) if C < HEAD_DIM_B else None

    def pad128(x):
        return x if pad_rows is None else jnp.concatenate([x, pad_rows], axis=0)

    def conv(i, col):
        acc = ext_ref[i, pl.ds(first_row, C), pl.ds(col, HEAD_DIM_B)] * cw_ref[0:1, pl.ds(col, HEAD_DIM_B)]
        for j in range(1, CONV_WIDTH):
            acc = acc + (ext_ref[i, pl.ds(first_row + j, C), pl.ds(col, HEAD_DIM_B)]
                         * cw_ref[j:j + 1, pl.ds(col, HEAD_DIM_B)])
        return _silu_of_half(acc)

    def l2n(x, scale=1.0):
        return x * (lax.rsqrt(jnp.sum(x * x, axis=-1, keepdims=True) + RMS_EPS) * scale)

    def gates(i):
        ab = ab_ref[i, 0]
        z = ab + dt_bias
        softplus = jnp.maximum(z, 0.0) + jnp.log(1.0 + jnp.exp(-jnp.abs(z)))
        g_all = neg_a * softplus
        beta_all = _sigmoid(ab)
        gcum_all = jnp.dot(tril, g_all, preferred_element_type=F32, precision=HIGHEST)
        return beta_all, gcum_all, pad128(gcum_all).T

    def body(it, carry):
        seqs = [it * per_iter + s for s in range(per_iter)]
        beta_all, gcum_all, gcum_rows = zip(*[gates(i) for i in seqs])
        chains = [(s, h) for s in range(per_iter) for h in range(N_HEADS_B)]
        each = lambda f: [f(s, h) for s, h in chains]
        n = range(len(chains))
        states = each(lambda s, h: s_prev[seqs[s], h])
        q = each(lambda s, h: l2n(conv(seqs[s], h * HEAD_DIM_B), HEAD_DIM_B ** -0.5))
        k = each(lambda s, h: l2n(conv(seqs[s], WIDTH_B + h * HEAD_DIM_B)))
        v = each(lambda s, h: conv(seqs[s], 2 * WIDTH_B + h * HEAD_DIM_B))
        beta = each(lambda s, h: beta_all[s][:, N_HEADS_B + h:N_HEADS_B + h + 1])
        gcum = each(lambda s, h: gcum_all[s][:, h:h + 1])
        grow = each(lambda s, h: gcum_rows[s][h:h + 1, 0:C])
        decay = [jnp.exp(jnp.where(causal, gcum[c] - grow[c], -jnp.inf)) for c in n]
        qk = [_bdot_nt(jnp.concatenate([q[c], k[c]], axis=0), k[c]) for c in n]
        intra = [qk[c][0:C] * decay[c] for c in n]
        lower = [jnp.where(strict, qk[c][C:2 * C] * beta[c] * decay[c], 0.0) for c in n]
        inv = _unit_lower_inverses(lower, ri, ci)
        e_g = [jnp.exp(gcum[c]) for c in n]
        uw = [_bdot(inv[c], jnp.concatenate([v[c] * beta[c], k[c] * (beta[c] * e_g[c])], axis=-1)) for c in n]
        ws_qs = [_bdot(jnp.concatenate([uw[c][:, HEAD_DIM_B:2 * HEAD_DIM_B], q[c] * e_g[c]], axis=0), states[c])
                 for c in n]
        v_new = [uw[c][:, 0:HEAD_DIM_B] - ws_qs[c][0:C] for c in n]
        g_last = [gcum[c][C - 1:C, :] for c in n]
        k_dec = [k[c] * jnp.exp(g_last[c] - gcum[c]) for c in n]
        o = [ws_qs[c][C:2 * C] + _bdot(intra[c], v_new[c]) for c in n]
        new_states = [states[c] * jnp.exp(g_last[c]) + _bdot(pad128(k_dec[c]).T, pad128(v_new[c])) for c in n]
        for c, (s, h) in enumerate(chains):
            s_ref[seqs[s], h] = new_states[c]
            o_c = o[c] * lax.rsqrt(jnp.mean(o[c] * o[c], axis=-1, keepdims=True) + RMS_EPS) * norm_w
            o_ref[seqs[s], :, pl.ds(h * HEAD_DIM_B, HEAD_DIM_B)] = o_c
        return carry

    lax.fori_loop(0, bb // per_iter, body, 0)


def _gated_deltanet(rest, att, conv0, s0, layer, conv_w, params, bb, per_iter, chunk, state_buf=None):
    nseq, t, _ = rest.shape
    qb = REST_QKVB // WIDTH_B
    col_spec = lambda col: pl.BlockSpec((bb, chunk, WIDTH_B), lambda i, c: (i, c, col))
    state_shape = (bb, N_HEADS_B, HEAD_DIM_B, HEAD_DIM_B)
    layered = s0.ndim == 5
    if layered:
        state_spec = pl.BlockSpec((1,) + state_shape, lambda i, c: (layer, i, 0, 0, 0))
        state_out = jax.ShapeDtypeStruct(s0.shape, F32)
    else:
        state_spec = pl.BlockSpec(state_shape, lambda i, c: (i, 0, 0, 0))
        state_out = jax.ShapeDtypeStruct((nseq,) + state_shape[1:], F32)
    in_specs = [col_spec(qb), col_spec(qb + 1), col_spec(qb + 2),
                pl.BlockSpec((bb, 1, chunk, LANES), lambda i, c: (i, AB_BLOCK, c, 0)),
                pl.BlockSpec((bb, SUBLANES, CONV_DIM), lambda i, c: (i, 0, 0)),
                state_spec,
                pl.BlockSpec((1, SUBLANES, CONV_DIM), lambda i, c: (layer, 0, 0)),
                pl.BlockSpec((1, SUBLANES, LANES), lambda i, c: (layer, 0, 0))]
    args = [rest, rest, rest, att, conv0, s0, conv_w, params]
    aliases = {}
    if state_buf is not None:
        aliases = {len(args): 1}
        in_specs.append(pl.BlockSpec(memory_space=pl.ANY))
        args.append(state_buf)
    return pl.pallas_call(
        functools.partial(_gdn_kernel, bb=bb, per_iter=per_iter, chunk=chunk, layered=layered,
                          single_chunk=(t == chunk), aliased=state_buf is not None),
        out_shape=[jax.ShapeDtypeStruct((nseq, t, WIDTH_B), F32), state_out],
        grid=(nseq // bb, t // chunk),
        in_specs=in_specs,
        out_specs=[pl.BlockSpec((bb, chunk, WIDTH_B), lambda i, c: (i, c, 0)), state_spec],
        scratch_shapes=[pltpu.VMEM((bb, chunk + SUBLANES, CONV_DIM), F32)],
        input_output_aliases=aliases,
        compiler_params=_cparams(("arbitrary", "arbitrary")),
        name="gated_deltanet",
    )(*args)


def _out_kernel(x_ref, mod_ref, o1_ref, o2_ref, o3_ref, l1_ref, l2_ref, l3_ref, za_ref, ob_ref, zb_ref, ga_ref, gb_ref,
                wa_ref, wb_ref, wo_ref, lng_ref, lnb_ref, y_ref):
    bs, bt, _ = x_ref.shape
    rows = bs * bt
    flat = lambda ref: ref[...].reshape(rows, ref.shape[-1]).astype(F32)
    group = lambda ref: jnp.concatenate([ref[:, half].reshape(rows, LANES) for half in range(HALVES)], axis=-1)
    l1, l2, l3 = group(l1_ref), group(l2_ref), group(l3_ref)
    m = jnp.maximum(jnp.maximum(l1, l2), l3)
    e1, e2, e3 = jnp.exp(l1 - m), jnp.exp(l2 - m), jnp.exp(l3 - m)
    inv_tot = 1.0 / (e1 + e2 + e3)
    att = jnp.concatenate([group(o1_ref) * (e1 * inv_tot), group(o2_ref) * (e2 * inv_tot),
                           group(o3_ref) * (e3 * inv_tot)], axis=-1)
    half_a = jnp.dot((att * _silu_of_half(flat(za_ref))).astype(BF16), wa_ref[0], preferred_element_type=F32)
    half_b = jnp.dot((flat(ob_ref) * _silu_of_half(flat(zb_ref))).astype(BF16), wb_ref[0], preferred_element_type=F32)
    merged = (half_a + half_a * jnp.tanh(flat(ga_ref))) + (half_b + half_b * jnp.tanh(flat(gb_ref)))
    out = jnp.dot(merged.astype(BF16), wo_ref[0], preferred_element_type=F32).reshape(bs, bt, D_MODEL)
    gate = mod_ref[:, :, 2 * D_MODEL:3 * D_MODEL]
    y = DEEPNORM_ALPHA * x_ref[...] + gate * out
    mu = jnp.mean(y, axis=-1, keepdims=True)
    yc = y - mu
    var = jnp.mean(yc * yc, axis=-1, keepdims=True)
    y_ref[...] = yc * lax.rsqrt(var + LN_EPS) * lng_ref[...] + lnb_ref[...]


def _output_block(x, mod, o_g, lse_g, o_b, rest, wa, wb, wo, ln_g, ln_b, layer, bs, bt):
    nseq, t, _ = x.shape
    row = lambda w, col: pl.BlockSpec((bs, bt, w), lambda i, k: (i, k, col))
    halves = pl.BlockSpec((bs, HALVES, bt, LANES), lambda i, k: (i, 0, k, 0))
    per_layer = lambda shape: pl.BlockSpec((1,) + shape, lambda i, k: (layer,) + (0,) * len(shape))
    return pl.pallas_call(
        _out_kernel,
        out_shape=jax.ShapeDtypeStruct((nseq, t, D_MODEL), F32),
        grid=(nseq // bs, t // bt),
        in_specs=[row(D_MODEL, 0),
                  pl.BlockSpec((bs, 1, 3 * D_MODEL), lambda i, k: (i, 0, 0)),
                  halves, halves, halves, halves, halves, halves,
                  row(WIDTH_A, REST_ZA // WIDTH_A),
                  row(WIDTH_B, 0),
                  row(WIDTH_B, REST_ZB // WIDTH_B),
                  row(D_MODEL, REST_GA // D_MODEL),
                  row(D_MODEL, REST_GB // D_MODEL),
                  per_layer((WIDTH_A, D_MODEL)), per_layer((WIDTH_B, D_MODEL)), per_layer((D_MODEL, D_MODEL)),
                  per_layer((1, D_MODEL)), per_layer((1, D_MODEL))],
        out_specs=row(D_MODEL, 0),
        compiler_params=_cparams(("arbitrary", "arbitrary")),
        name="output_block",
    )(x, mod, o_g[0], o_g[1], o_g[2], lse_g[0], lse_g[1], lse_g[2], rest, o_b, rest, rest, rest,
      wa, wb, wo, ln_g, ln_b)


def _alibi_slopes():
    i = jnp.arange(1, N_HEADS_A + 1, dtype=F32)
    return jnp.exp2(-8.0 * i / N_HEADS_A).reshape(N_GROUPS, HEADS_PER_GROUP)


def _split_w_in(w_in):
    zeros = lambda n: jnp.zeros(w_in.shape[:2] + (n,), BF16)
    att = jnp.concatenate([w_in[:, :, :SRC_ZA].astype(BF16), w_in[:, :, SRC_AB:SRC_GA].astype(BF16),
                           zeros(LANES - 2 * N_HEADS_B)], axis=-1)
    rest = jnp.concatenate([w_in[:, :, SRC_ZA:SRC_AB].astype(BF16), zeros(REST_GA - (SRC_AB - SRC_ZA)),
                            w_in[:, :, SRC_GA:].astype(BF16)], axis=-1)
    return att, rest


def _split_kv(att, group, start):
    b = att.shape[0]
    k = att[:, (N_GROUPS + group) * HALVES:(N_GROUPS + group + 1) * HALVES, start:, :]
    v = att[:, (2 * N_GROUPS + group) * HALVES:(2 * N_GROUPS + group + 1) * HALVES, start:, :]
    kv = jnp.stack([k, v], axis=1)
    kv = kv.transpose(0, 3, 1, 2, 4)
    return kv.reshape(b, kv.shape[1], 2, HEADS_PER_GROUP, HEAD_DIM_A)


def _feature_major(cache):
    d, b, l = cache.shape[:3]
    return cache.transpose(0, 1, 3, 4, 5, 2).reshape(d, b, 2, GROUP_W, l)


def kernel(x_prompt, x_sample, c_prompt, c_sample, cache_win1, cache_win2, cache_win3, state_gdn, state_conv,
           w_ada, b_ada, w_in, conv_w, a_log, dt_bias, gdn_norm, w_out_a, w_out_b, w_o, ln_g, ln_b):
    n_p, seq, _ = x_prompt.shape
    n_s, t_new, _ = x_sample.shape
    caches = tuple(_feature_major(c) for c in (cache_win1, cache_win2, cache_win3))
    sample_attn_tiling = ((8, 4), (8, 2), (2, 2))
    slopes = _alibi_slopes()

    w_att, w_rest = _split_w_in(w_in)
    wa = (0.5 * w_out_a).astype(BF16)
    wb = (0.5 * w_out_b).astype(BF16)
    wo = w_o.astype(BF16)
    ln_g3 = ln_g.reshape(DEPTH, 1, D_MODEL)
    ln_b3 = ln_b.reshape(DEPTH, 1, D_MODEL)
    conv_w_p = jnp.pad(0.5 * conv_w, ((0, 0), (0, SUBLANES - CONV_WIDTH), (0, 0)))
    params = jnp.zeros((DEPTH, SUBLANES, LANES), F32)
    params = params.at[:, 0, :N_HEADS_B].set(a_log).at[:, 1, :N_HEADS_B].set(dt_bias).at[:, 2, :].set(gdn_norm)

    n_all = n_p + n_s
    n_pad = -(-n_all // SUBLANES) * SUBLANES
    c_all = jnp.pad(jnp.concatenate([c_prompt, c_sample], axis=0), ((0, n_pad - n_all), (0, 0)))
    mod_all = _modulation(c_all, w_ada, b_ada)

    zero_conv = jnp.zeros((n_p, SUBLANES, CONV_DIM), F32)
    zero_state = jnp.zeros((n_p, N_HEADS_B, HEAD_DIM_B, HEAD_DIM_B), F32)
    conv_s = jnp.pad(state_conv, ((0, 0), (0, 0), (SUBLANES - (CONV_WIDTH - 1), 0), (0, 0)))

    xp, xs = x_prompt, x_sample
    acc_p = [[] for _ in range(N_GROUPS + 2)]
    acc_s = [[] for _ in range(N_GROUPS + 2)]
    gdn_sample = None
    for l in range(DEPTH):
        mod_p = mod_all[l, :n_p, None, :]
        mod_s = mod_all[l, n_p:n_all, None, :]

        att_p, rest_p = _in_projection(xp, mod_p, w_att, w_rest, l, 1, PROMPT_ROW_TILE, BF16)
        att_s, rest_s = _in_projection(xs, mod_s, w_att, w_rest, l, SAMPLE_SEQ_TILE, t_new, F32)

        o_p, lse_p, o_s, lse_s = [], [], [], []
        for g in range(N_GROUPS):
            o, lse = _attention_prompt(att_p, slopes, g)
            o_p.append(o)
            lse_p.append(lse)
            o, lse = _attention_sample(att_s, caches[g], l, slopes, g, *sample_attn_tiling[g])
            o_s.append(o)
            lse_s.append(lse)

        ob_p, st_p = _gated_deltanet(rest_p, att_p, zero_conv, zero_state, l, conv_w_p, params, PROMPT_GDN_SEQS,
                                     PROMPT_GDN_SEQS, PROMPT_GDN_CHUNK)
        ob_s, gdn_sample = _gated_deltanet(rest_s, att_s, conv_s[l], state_gdn, l, conv_w_p, params,
                                           *SAMPLE_GDN_TILING, t_new, state_buf=gdn_sample)

        xp_new = _output_block(xp, mod_p, o_p, lse_p, ob_p, rest_p, wa, wb, wo, ln_g3, ln_b3, l,
                               1, PROMPT_OUT_ROW_TILE)
        xs_new = _output_block(xs, mod_s, o_s, lse_s, ob_s, rest_s, wa, wb, wo, ln_g3, ln_b3, l,
                               SAMPLE_SEQ_TILE, t_new)

        for g in range(N_GROUPS):
            acc_p[g].append(_split_kv(att_p, g, seq - min(WINDOWS[g], seq)))
            acc_s[g].append(_split_kv(att_s, g, 0))
        acc_p[N_GROUPS].append(st_p)
        acc_p[N_GROUPS + 1].append(rest_p[:, seq - (CONV_WIDTH - 1):, REST_QKVB:REST_QKVB + CONV_DIM].astype(F32))
        acc_s[N_GROUPS + 1].append(rest_s[:, t_new - (CONV_WIDTH - 1):, REST_QKVB:REST_QKVB + CONV_DIM])
        xp, xs = xp_new, xs_new

    acc_s[N_GROUPS] = None
    outs_p = [jnp.stack(a) for a in acc_p]
    outs_s = [gdn_sample if a is None else jnp.stack(a) for a in acc_s]
    return (xp, xs, *outs_p, *outs_s)
```

```python
import functools

import numpy as np
import jax
import jax.numpy as jnp
from jax import lax
from jax.experimental import pallas as pl
from jax.experimental.pallas import tpu as pltpu

F32 = jnp.float32
BF16 = jnp.bfloat16
HIGHEST = lax.Precision.HIGHEST

D_MODEL = 1024
DEPTH = 4
DILATIONS = (1, 4, 16)
WINDOWS = (128, 512, 2048)
N_GROUPS = 3
HEADS_PER_GROUP = 4
HEAD_DIM_A = 64
GROUP_W = HEADS_PER_GROUP * HEAD_DIM_A
WIDTH_A = N_GROUPS * GROUP_W
N_HEADS_A = N_GROUPS * HEADS_PER_GROUP
STEPS_BACK = 128
A_BLOCK = 128
N_HEADS_B = 6
HEAD_DIM_B = 128
WIDTH_B = N_HEADS_B * HEAD_DIM_B
CONV_WIDTH = 4
CONV_DIM = 3 * WIDTH_B
DEEPNORM_ALPHA = (2 * DEPTH) ** 0.25
LN_EPS = 1e-5
RMS_EPS = 1e-6
LANES = 128
SUBLANES = 8
HALVES = GROUP_W // LANES

SRC_ZA = 3 * WIDTH_A
SRC_QKVB = SRC_ZA + WIDTH_A
SRC_ZB = SRC_QKVB + CONV_DIM
SRC_AB = SRC_ZB + WIDTH_B
SRC_GA = SRC_AB + 2 * N_HEADS_B
SRC_GB = SRC_GA + D_MODEL

AB_BLOCK = 3 * WIDTH_A // LANES
ATT_BLOCKS = AB_BLOCK + 1
ATT_W = ATT_BLOCKS * LANES
REST_ZA = 0
REST_QKVB = 768
REST_ZB = 3072
REST_GA = 4096
REST_GB = 5120
REST_W = 6144
REST_TILE = 2048

PROMPT_GDN_CHUNK = 128
PROMPT_GDN_SEQS = 2
TRI_BASE = 16
ATTN_TASKS = 2
ATTN_STATIC_BLOCKS = 4
PROMPT_ROW_TILE = 512
PROMPT_PROJ_ROW_TILE = 1024
PROMPT_OUT_ROW_TILE = 512
SAMPLE_SEQ_TILE = 32
SAMPLE_GDN_TILING = (8, 4)
VMEM_LIMIT = 48 * 1024 * 1024

NT_DIMS = (((1,), (1,)), ((), ()))


def _cparams(sem):
    return pltpu.CompilerParams(dimension_semantics=sem, vmem_limit_bytes=VMEM_LIMIT)


def _sigmoid(x):
    return 0.5 * jnp.tanh(0.5 * x) + 0.5


def _silu_of_half(h):
    return h + h * jnp.tanh(h)


def _silu(x):
    return _silu_of_half(0.5 * x)


def _bdot(a, b):
    return jnp.dot(a.astype(BF16), b.astype(BF16), preferred_element_type=F32)


def _bdot_nt(a, b):
    return lax.dot_general(a.astype(BF16), b.astype(BF16), NT_DIMS, preferred_element_type=F32)


def _mod_kernel(c_ref, w_ref, b_ref, o_ref):
    o_ref[0] = _bdot(_silu(c_ref[...]), w_ref[0]) + b_ref[0]


def _modulation(c_all, w_ada, b_ada):
    n = c_all.shape[0]
    return pl.pallas_call(
        _mod_kernel,
        out_shape=jax.ShapeDtypeStruct((DEPTH, n, 3 * D_MODEL), F32),
        grid=(DEPTH, 3),
        in_specs=[
            pl.BlockSpec((n, D_MODEL), lambda l, j: (0, 0)),
            pl.BlockSpec((1, D_MODEL, D_MODEL), lambda l, j: (l, 0, j)),
            pl.BlockSpec((1, 1, D_MODEL), lambda l, j: (l, 0, j)),
        ],
        out_specs=pl.BlockSpec((1, n, D_MODEL), lambda l, j: (l, 0, j)),
        compiler_params=_cparams(("arbitrary", "arbitrary")),
        name="adaln_modulation",
    )(c_all, w_ada, b_ada.reshape(DEPTH, 1, 3 * D_MODEL))


def _modulated(x_ref, mod_ref):
    bs, bt, _ = x_ref.shape
    shift = mod_ref[:, :, 0:D_MODEL]
    scale = mod_ref[:, :, D_MODEL:2 * D_MODEL]
    h = x_ref[...] * (1.0 + scale) + shift
    return h.reshape(bs * bt, D_MODEL).astype(BF16)


def _inproj_att_kernel(x_ref, mod_ref, w_ref, o_ref):
    bs, bt, _ = x_ref.shape
    res = jnp.dot(_modulated(x_ref, mod_ref), w_ref[0], preferred_element_type=F32)
    for j in range(ATT_BLOCKS):
        o_ref[:, j] = res[:, j * LANES:(j + 1) * LANES].reshape(bs, bt, LANES)


def _inproj_rest_kernel(x_ref, mod_ref, w_ref, scale_ref, o_ref):
    bs, bt, _ = x_ref.shape
    res = jnp.dot(_modulated(x_ref, mod_ref), w_ref[0], preferred_element_type=F32)
    res = res * scale_ref[...]
    o_ref[...] = res.reshape(bs, bt, REST_TILE).astype(o_ref.dtype)


def _rest_column_scale():
    scale = np.full((1, REST_W), 0.5, np.float32)
    scale[:, REST_QKVB:REST_ZB] = 1.0
    return jnp.asarray(scale)


def _in_projection(x, mod, w_att, w_rest, layer, bs, bt, rest_dtype):
    nseq, t, _ = x.shape
    att = pl.pallas_call(
        _inproj_att_kernel,
        out_shape=jax.ShapeDtypeStruct((nseq, ATT_BLOCKS, t, LANES), F32),
        grid=(nseq // bs, t // bt),
        in_specs=[
            pl.BlockSpec((bs, bt, D_MODEL), lambda i, k: (i, k, 0)),
            pl.BlockSpec((bs, 1, 3 * D_MODEL), lambda i, k: (i, 0, 0)),
            pl.BlockSpec((1, D_MODEL, ATT_W), lambda i, k: (layer, 0, 0)),
        ],
        out_specs=pl.BlockSpec((bs, ATT_BLOCKS, bt, LANES), lambda i, k: (i, 0, k, 0)),
        compiler_params=_cparams(("arbitrary", "arbitrary")),
        name="in_projection_att",
    )(x, mod, w_att)
    rest = pl.pallas_call(
        _inproj_rest_kernel,
        out_shape=jax.ShapeDtypeStruct((nseq, t, REST_W), rest_dtype),
        grid=(REST_W // REST_TILE, nseq // bs, t // bt),
        in_specs=[
            pl.BlockSpec((bs, bt, D_MODEL), lambda j, i, k: (i, k, 0)),
            pl.BlockSpec((bs, 1, 3 * D_MODEL), lambda j, i, k: (i, 0, 0)),
            pl.BlockSpec((1, D_MODEL, REST_TILE), lambda j, i, k: (layer, 0, j)),
            pl.BlockSpec((1, REST_TILE), lambda j, i, k: (0, j)),
        ],
        out_specs=pl.BlockSpec((bs, bt, REST_TILE), lambda j, i, k: (i, k, j)),
        compiler_params=_cparams(("arbitrary", "arbitrary", "arbitrary")),
        name="in_projection_rest",
    )(x, mod, w_rest, _rest_column_scale())
    return att, rest


def _attn_prompt_kernel(slopes_ref, q_ref, kp_ref, ko_ref, vp_ref, vo_ref, o_ref, lse_ref, *, group, dilation, units,
                        tasks):
    n = pl.program_id(1)
    span = A_BLOCK * dilation
    qi = lax.broadcasted_iota(jnp.int32, (A_BLOCK, 2 * A_BLOCK), 0)
    kj = lax.broadcasted_iota(jnp.int32, (A_BLOCK, 2 * A_BLOCK), 1)
    steps = A_BLOCK + qi - kj
    band = (steps >= 0) & (steps <= STEPS_BACK)
    dist = (steps * dilation).astype(F32)
    alibi = [jnp.where(band, -slopes_ref[group, h] * dist, -jnp.inf) for h in range(HEADS_PER_GROUP)]
    no_prev = jnp.where((kj >= A_BLOCK) | (n > 0), 0.0, -jnp.inf)
    lane_head = lax.broadcasted_iota(jnp.int32, (1, GROUP_W), 1) // HEAD_DIM_A
    heads = range(HEADS_PER_GROUP)

    def rows_of(ref, rows):
        return jnp.concatenate([ref[0, half, rows, :] for half in range(HALVES)], axis=-1)

    def run(tasks):
        q, k, v, own_rows = [], [], [], []
        for u, r in tasks:
            own = pl.ds(u * span + r, A_BLOCK, stride=dilation)
            if u == 0:
                prev_k, prev_v, prev = kp_ref, vp_ref, pl.ds(r, A_BLOCK, stride=dilation)
            else:
                prev_k, prev_v, prev = ko_ref, vo_ref, pl.ds((u - 1) * span + r, A_BLOCK, stride=dilation)
            own_rows.append(own)
            q.append(rows_of(q_ref, own) * (HEAD_DIM_A ** -0.5))
            k.append(jnp.concatenate([rows_of(prev_k, prev), rows_of(ko_ref, own)], axis=0).astype(BF16))
            v.append(jnp.concatenate([rows_of(prev_v, prev), rows_of(vo_ref, own)], axis=0).astype(BF16))
        chains = [(t, h) for t in range(len(tasks)) for h in heads]
        s = [lax.dot_general(jnp.where(lane_head == h, q[t], 0.0).astype(BF16), k[t], NT_DIMS,
                             preferred_element_type=F32) for t, h in chains]
        s = [s[c] + alibi[h] for c, (t, h) in enumerate(chains)]
        s = [s[c] + no_prev if tasks[t][0] == 0 else s[c] for c, (t, h) in enumerate(chains)]
        m = [x.max(-1, keepdims=True) for x in s]
        p = [jnp.exp(x - mx) for x, mx in zip(s, m)]
        den = [x.sum(-1, keepdims=True) for x in p]
        oh = [jnp.dot((p[c] / den[c]).astype(BF16), v[t], preferred_element_type=F32) for c, (t, h) in enumerate(chains)]
        for t in range(len(tasks)):
            o_acc = jnp.zeros((A_BLOCK, GROUP_W), F32)
            lse_acc = jnp.zeros((A_BLOCK, GROUP_W), F32)
            for h in heads:
                c = t * HEADS_PER_GROUP + h
                o_acc = jnp.where(lane_head == h, oh[c], o_acc)
                lse_acc = jnp.where(lane_head == h, m[c] + jnp.log(den[c]), lse_acc)
            for half in range(HALVES):
                o_ref[0, half, own_rows[t], :] = o_acc[:, half * LANES:(half + 1) * LANES]
                lse_ref[0, half, own_rows[t], :] = lse_acc[:, half * LANES:(half + 1) * LANES]

    if dilation * units <= ATTN_STATIC_BLOCKS:
        blocks = [(u, r) for u in range(units) for r in range(dilation)]
        for j in range(0, len(blocks), tasks):
            run(blocks[j:j + tasks])
    else:
        for u in range(units):
            def residues(j, carry, u=u):
                run([(u, j * tasks + t) for t in range(tasks)])
                return carry
            lax.fori_loop(0, dilation // tasks, residues, 0)


def _attention_prompt(att, slopes, group):
    b, _, s, _ = att.shape
    d = DILATIONS[group]
    span = A_BLOCK * d
    units = max(1, PROMPT_ROW_TILE // span)
    rows = span * units
    tasks = ATTN_TASKS * (2 if d > SUBLANES else 1)
    qb, kb, vb = group, N_GROUPS + group, 2 * N_GROUPS + group
    own = lambda col: pl.BlockSpec((1, HALVES, rows, LANES), lambda i, n: (i, col, n, 0))
    prev = lambda col: pl.BlockSpec((1, HALVES, span, LANES), lambda i, n: (i, col, jnp.maximum(n * units - 1, 0), 0))
    out_spec = pl.BlockSpec((1, HALVES, rows, LANES), lambda i, n: (i, 0, n, 0))
    return pl.pallas_call(
        functools.partial(_attn_prompt_kernel, group=group, dilation=d, units=units, tasks=tasks),
        out_shape=[jax.ShapeDtypeStruct((b, HALVES, s, LANES), F32)] * 2,
        grid=(b, s // rows),
        in_specs=[pl.BlockSpec(memory_space=pltpu.SMEM), own(qb), prev(kb), own(kb), prev(vb), own(vb)],
        out_specs=[out_spec, out_spec],
        compiler_params=_cparams(("arbitrary", "arbitrary")),
        name=f"attn_prompt_g{group}",
    )(slopes, att, att, att, att, att)


def _sample_bias(slopes, group, t_new):
    d = DILATIONS[group]
    lb = WINDOWS[group]
    t = np.arange(t_new)

    def table(key_pos, width):
        diff = (lb + t)[:, None] - key_pos[None, :]
        ok = (diff >= 0) & (diff % d == 0) & (diff // d <= STEPS_BACK)
        dist = jnp.asarray(np.where(ok, diff, 0), F32)
        bias = jnp.where(jnp.asarray(ok)[None], -slopes[group][:, None, None] * dist[None], -jnp.inf)
        bias = bias.reshape(HEADS_PER_GROUP * t_new, key_pos.shape[0])
        return jnp.pad(bias, ((0, 0), (0, width - key_pos.shape[0])), constant_values=-jnp.inf)

    return table(np.arange(lb), lb), table(lb + t, LANES)


def _attn_sample_kernel(q_ref, k_ref, v_ref, cache_ref, bias_ref, biasn_ref, o_ref, lse_ref, *, bb, per_iter, t_new):
    nq = HEADS_PER_GROUP * t_new
    row_head = lax.broadcasted_iota(jnp.int32, (nq, GROUP_W), 0) // t_new
    lane_head2 = lax.broadcasted_iota(jnp.int32, (nq, GROUP_W), 1) // HEAD_DIM_A
    lane_head = lax.broadcasted_iota(jnp.int32, (1, GROUP_W), 1) // HEAD_DIM_A
    pad_n = jnp.zeros((LANES - t_new, GROUP_W), F32)

    def new_rows(ref, i):
        return jnp.concatenate([ref[i, half] for half in range(HALVES)], axis=-1)

    def body(it, carry):
        seqs = [it * per_iter + s for s in range(per_iter)]
        each = lambda f: [f(i) for i in seqs]
        n = range(per_iter)
        q = each(lambda i: new_rows(q_ref, i) * (HEAD_DIM_A ** -0.5))
        qm = [jnp.where(row_head == lane_head2, jnp.concatenate([x] * HEADS_PER_GROUP, axis=0), 0.0).astype(BF16)
              for x in q]
        kt = each(lambda i: cache_ref[0, i, 0].astype(BF16))
        vt = each(lambda i: cache_ref[0, i, 1].astype(BF16))
        kn = each(lambda i: jnp.concatenate([new_rows(k_ref, i), pad_n], axis=0).astype(BF16))
        vn = each(lambda i: jnp.concatenate([new_rows(v_ref, i), pad_n], axis=0).astype(BF16))
        sc = [jnp.dot(qm[c], kt[c], preferred_element_type=F32) + bias_ref[...] for c in n]
        sn = [lax.dot_general(qm[c], kn[c], NT_DIMS, preferred_element_type=F32) + biasn_ref[...] for c in n]
        m = [jnp.maximum(sc[c].max(-1, keepdims=True), sn[c].max(-1, keepdims=True)) for c in n]
        pc = [jnp.exp(sc[c] - m[c]) for c in n]
        pn = [jnp.exp(sn[c] - m[c]) for c in n]
        den = [pc[c].sum(-1, keepdims=True) + pn[c].sum(-1, keepdims=True) for c in n]
        of = [lax.dot_general((pc[c] / den[c]).astype(BF16), vt[c], NT_DIMS, preferred_element_type=F32)
              + jnp.dot((pn[c] / den[c]).astype(BF16), vn[c], preferred_element_type=F32) for c in n]
        for c, i in enumerate(seqs):
            lse_col = m[c] + jnp.log(den[c])
            o = jnp.zeros((t_new, GROUP_W), F32)
            lse = jnp.zeros((t_new, GROUP_W), F32)
            for h in range(HEADS_PER_GROUP):
                head = lane_head == h
                o = jnp.where(head, of[c][h * t_new:(h + 1) * t_new, :], o)
                lse = jnp.where(head, lse_col[h * t_new:(h + 1) * t_new, :], lse)
            for half in range(HALVES):
                o_ref[i, half] = o[:, half * LANES:(half + 1) * LANES]
                lse_ref[i, half] = lse[:, half * LANES:(half + 1) * LANES]
        return carry

    lax.fori_loop(0, bb // per_iter, body, 0)


def _attention_sample(att, cache_t, layer, slopes, group, bb, per_iter):
    nseq, _, t_new, _ = att.shape
    lb = WINDOWS[group]
    bias, bias_new = _sample_bias(slopes, group, t_new)
    qb, kb, vb = group, N_GROUPS + group, 2 * N_GROUPS + group
    new_spec = lambda col: pl.BlockSpec((bb, HALVES, t_new, LANES), lambda i: (i, col, 0, 0))
    out_spec = pl.BlockSpec((bb, HALVES, t_new, LANES), lambda i: (i, 0, 0, 0))
    return pl.pallas_call(
        functools.partial(_attn_sample_kernel, bb=bb, per_iter=per_iter, t_new=t_new),
        out_shape=[jax.ShapeDtypeStruct((nseq, HALVES, t_new, LANES), F32)] * 2,
        grid=(nseq // bb,),
        in_specs=[new_spec(qb), new_spec(kb), new_spec(vb),
                  pl.BlockSpec((1, bb, 2, GROUP_W, lb), lambda i: (layer, i, 0, 0, 0)),
                  pl.BlockSpec(bias.shape, lambda i: (0, 0)), pl.BlockSpec(bias_new.shape, lambda i: (0, 0))],
        out_specs=[out_spec, out_spec],
        compiler_params=_cparams(("arbitrary",)),
        name=f"attn_sample_g{group}",
    )(att, att, att, cache_t, bias, bias_new)


def _unit_lower_inverses(lowers, ri, ci):
    c = lowers[0].shape[0]
    base = min(TRI_BASE, c)
    ident = (ri == ci).astype(F32)
    same = (ri // base) == (ci // base)
    powers = [jnp.where(same, l, 0.0) for l in lowers]
    invs = [ident - p for p in powers]
    for _ in range(int(np.log2(base)) - 1):
        powers = [_bdot(p, p) for p in powers]
        invs = [t + _bdot(t, p) for t, p in zip(invs, powers)]
    size = base
    while size < c:
        pair = ((ri // (2 * size)) == (ci // (2 * size))) & ((ri // size) % 2 == 1) & ((ci // size) % 2 == 0)
        offs = [_bdot(jnp.where(pair, l, 0.0), t) for l, t in zip(lowers, invs)]
        invs = [t - _bdot(t, o) for t, o in zip(invs, offs)]
        size *= 2
    return invs


def _gdn_kernel(*refs, bb, per_iter, chunk, layered, single_chunk, aliased):
    (q_ref, k_ref, v_ref, ab_ref, conv0_ref, s0_ref, cw_ref, par_ref), refs = refs[:8], refs[8 + int(aliased):]
    o_ref, s_ref, ext_ref = refs
    c_idx = pl.program_id(1)
    C = chunk
    first_row = SUBLANES - (CONV_WIDTH - 1)
    if layered:
        s0_ref, s_ref = s0_ref.at[0], s_ref.at[0]
    cw_ref, par_ref = cw_ref.at[0], par_ref.at[0]
    s_prev = s0_ref if single_chunk else s_ref

    @pl.when(c_idx == 0)
    def _():
        ext_ref[:, 0:SUBLANES, :] = conv0_ref[...]
        if not single_chunk:
            s_ref[...] = s0_ref[...]

    @pl.when(c_idx > 0)
    def _():
        ext_ref[:, 0:SUBLANES, :] = ext_ref[:, C:C + SUBLANES, :]

    ext_ref[:, SUBLANES:SUBLANES + C, 0:WIDTH_B] = q_ref[...].astype(F32)
    ext_ref[:, SUBLANES:SUBLANES + C, WIDTH_B:2 * WIDTH_B] = k_ref[...].astype(F32)
    ext_ref[:, SUBLANES:SUBLANES + C, 2 * WIDTH_B:3 * WIDTH_B] = v_ref[...].astype(F32)

    ri = lax.broadcasted_iota(jnp.int32, (C, C), 0)
    ci = lax.broadcasted_iota(jnp.int32, (C, C), 1)
    causal = ri >= ci
    strict = ri > ci
    tril = causal.astype(F32)
    neg_a = -jnp.exp(par_ref[0:1, :])
    dt_bias = par_ref[1:2, :]
    norm_w = par_ref[2:3, :]
    pad_rows = jnp.zeros((HEAD_DIM_B - C, HEAD_DIM_B), F32) if C < HEAD_DIM_B else None

    def pad128(x):
        return x if pad_rows is None else jnp.concatenate([x, pad_rows], axis=0)

    def conv(i, col):
        acc = ext_ref[i, pl.ds(first_row, C), pl.ds(col, HEAD_DIM_B)] * cw_ref[0:1, pl.ds(col, HEAD_DIM_B)]
        for j in range(1, CONV_WIDTH):
            acc = acc + (ext_ref[i, pl.ds(first_row + j, C), pl.ds(col, HEAD_DIM_B)]
                         * cw_ref[j:j + 1, pl.ds(col, HEAD_DIM_B)])
        return _silu_of_half(acc)

    def l2n(x, scale=1.0):
        return x * (lax.rsqrt(jnp.sum(x * x, axis=-1, keepdims=True) + RMS_EPS) * scale)

    def gates(i):
        ab = ab_ref[i, 0]
        z = ab + dt_bias
        softplus = jnp.maximum(z, 0.0) + jnp.log(1.0 + jnp.exp(-jnp.abs(z)))
        g_all = neg_a * softplus
        beta_all = _sigmoid(ab)
        gcum_all = jnp.dot(tril, g_all, preferred_element_type=F32, precision=HIGHEST)
        return beta_all, gcum_all, pad128(gcum_all).T

    def body(it, carry):
        seqs = [it * per_iter + s for s in range(per_iter)]
        beta_all, gcum_all, gcum_rows = zip(*[gates(i) for i in seqs])
        chains = [(s, h) for s in range(per_iter) for h in range(N_HEADS_B)]
        each = lambda f: [f(s, h) for s, h in chains]
        n = range(len(chains))
        states = each(lambda s, h: s_prev[seqs[s], h])
        q = each(lambda s, h: l2n(conv(seqs[s], h * HEAD_DIM_B), HEAD_DIM_B ** -0.5))
        k = each(lambda s, h: l2n(conv(seqs[s], WIDTH_B + h * HEAD_DIM_B)))
        v = each(lambda s, h: conv(seqs[s], 2 * WIDTH_B + h * HEAD_DIM_B))
        beta = each(lambda s, h: beta_all[s][:, N_HEADS_B + h:N_HEADS_B + h + 1])
        gcum = each(lambda s, h: gcum_all[s][:, h:h + 1])
        grow = each(lambda s, h: gcum_rows[s][h:h + 1, 0:C])
        decay = [jnp.exp(jnp.where(causal, gcum[c] - grow[c], -jnp.inf)) for c in n]
        qk = [_bdot_nt(jnp.concatenate([q[c], k[c]], axis=0), k[c]) for c in n]
        intra = [qk[c][0:C] * decay[c] for c in n]
        lower = [jnp.where(strict, qk[c][C:2 * C] * beta[c] * decay[c], 0.0) for c in n]
        inv = _unit_lower_inverses(lower, ri, ci)
        e_g = [jnp.exp(gcum[c]) for c in n]
        uw = [_bdot(inv[c], jnp.concatenate([v[c] * beta[c], k[c] * (beta[c] * e_g[c])], axis=-1)) for c in n]
        ws_qs = [_bdot(jnp.concatenate([uw[c][:, HEAD_DIM_B:2 * HEAD_DIM_B], q[c] * e_g[c]], axis=0), states[c])
                 for c in n]
        v_new = [uw[c][:, 0:HEAD_DIM_B] - ws_qs[c][0:C] for c in n]
        g_last = [gcum[c][C - 1:C, :] for c in n]
        k_dec = [k[c] * jnp.exp(g_last[c] - gcum[c]) for c in n]
        o = [ws_qs[c][C:2 * C] + _bdot(intra[c], v_new[c]) for c in n]
        new_states = [states[c] * jnp.exp(g_last[c]) + _bdot(pad128(k_dec[c]).T, pad128(v_new[c])) for c in n]
        for c, (s, h) in enumerate(chains):
            s_ref[seqs[s], h] = new_states[c]
            o_c = o[c] * lax.rsqrt(jnp.mean(o[c] * o[c], axis=-1, keepdims=True) + RMS_EPS) * norm_w
            o_ref[seqs[s], :, pl.ds(h * HEAD_DIM_B, HEAD_DIM_B)] = o_c
        return carry

    lax.fori_loop(0, bb // per_iter, body, 0)


def _gated_deltanet(rest, att, conv0, s0, layer, conv_w, params, bb, per_iter, chunk, state_buf=None):
    nseq, t, _ = rest.shape
    qb = REST_QKVB // WIDTH_B
    col_spec = lambda col: pl.BlockSpec((bb, chunk, WIDTH_B), lambda i, c: (i, c, col))
    state_shape = (bb, N_HEADS_B, HEAD_DIM_B, HEAD_DIM_B)
    layered = s0.ndim == 5
    if layered:
        state_spec = pl.BlockSpec((1,) + state_shape, lambda i, c: (layer, i, 0, 0, 0))
        state_out = jax.ShapeDtypeStruct(s0.shape, F32)
    else:
        state_spec = pl.BlockSpec(state_shape, lambda i, c: (i, 0, 0, 0))
        state_out = jax.ShapeDtypeStruct((nseq,) + state_shape[1:], F32)
    in_specs = [col_spec(qb), col_spec(qb + 1), col_spec(qb + 2),
                pl.BlockSpec((bb, 1, chunk, LANES), lambda i, c: (i, AB_BLOCK, c, 0)),
                pl.BlockSpec((bb, SUBLANES, CONV_DIM), lambda i, c: (i, 0, 0)),
                state_spec,
                pl.BlockSpec((1, SUBLANES, CONV_DIM), lambda i, c: (layer, 0, 0)),
                pl.BlockSpec((1, SUBLANES, LANES), lambda i, c: (layer, 0, 0))]
    args = [rest, rest, rest, att, conv0, s0, conv_w, params]
    aliases = {}
    if state_buf is not None:
        aliases = {len(args): 1}
        in_specs.append(pl.BlockSpec(memory_space=pl.ANY))
        args.append(state_buf)
    return pl.pallas_call(
        functools.partial(_gdn_kernel, bb=bb, per_iter=per_iter, chunk=chunk, layered=layered,
                          single_chunk=(t == chunk), aliased=state_buf is not None),
        out_shape=[jax.ShapeDtypeStruct((nseq, t, WIDTH_B), F32), state_out],
        grid=(nseq // bb, t // chunk),
        in_specs=in_specs,
        out_specs=[pl.BlockSpec((bb, chunk, WIDTH_B), lambda i, c: (i, c, 0)), state_spec],
        scratch_shapes=[pltpu.VMEM((bb, chunk + SUBLANES, CONV_DIM), F32)],
        input_output_aliases=aliases,
        compiler_params=_cparams(("arbitrary", "arbitrary")),
        name="gated_deltanet",
    )(*args)


def _out_kernel(x_ref, mod_ref, o1_ref, o2_ref, o3_ref, l1_ref, l2_ref, l3_ref, za_ref, ob_ref, zb_ref, ga_ref, gb_ref,
                wa_ref, wb_ref, wo_ref, lng_ref, lnb_ref, y_ref):
    bs, bt, _ = x_ref.shape
    rows = bs * bt
    flat = lambda ref: ref[...].reshape(rows, ref.shape[-1]).astype(F32)
    group = lambda ref: jnp.concatenate([ref[:, half].reshape(rows, LANES) for half in range(HALVES)], axis=-1)
    l1, l2, l3 = group(l1_ref), group(l2_ref), group(l3_ref)
    m = jnp.maximum(jnp.maximum(l1, l2), l3)
    e1, e2, e3 = jnp.exp(l1 - m), jnp.exp(l2 - m), jnp.exp(l3 - m)
    inv_tot = 1.0 / (e1 + e2 + e3)
    att = jnp.concatenate([group(o1_ref) * (e1 * inv_tot), group(o2_ref) * (e2 * inv_tot),
                           group(o3_ref) * (e3 * inv_tot)], axis=-1)
    half_a = jnp.dot((att * _silu_of_half(flat(za_ref))).astype(BF16), wa_ref[0], preferred_element_type=F32)
    half_b = jnp.dot((flat(ob_ref) * _silu_of_half(flat(zb_ref))).astype(BF16), wb_ref[0], preferred_element_type=F32)
    merged = (half_a + half_a * jnp.tanh(flat(ga_ref))) + (half_b + half_b * jnp.tanh(flat(gb_ref)))
    out = jnp.dot(merged.astype(BF16), wo_ref[0], preferred_element_type=F32).reshape(bs, bt, D_MODEL)
    gate = mod_ref[:, :, 2 * D_MODEL:3 * D_MODEL]
    y = DEEPNORM_ALPHA * x_ref[...] + gate * out
    mu = jnp.mean(y, axis=-1, keepdims=True)
    yc = y - mu
    var = jnp.mean(yc * yc, axis=-1, keepdims=True)
    y_ref[...] = yc * lax.rsqrt(var + LN_EPS) * lng_ref[...] + lnb_ref[...]


def _output_block(x, mod, o_g, lse_g, o_b, rest, wa, wb, wo, ln_g, ln_b, layer, bs, bt):
    nseq, t, _ = x.shape
    row = lambda w, col: pl.BlockSpec((bs, bt, w), lambda i, k: (i, k, col))
    halves = pl.BlockSpec((bs, HALVES, bt, LANES), lambda i, k: (i, 0, k, 0))
    per_layer = lambda shape: pl.BlockSpec((1,) + shape, lambda i, k: (layer,) + (0,) * len(shape))
    return pl.pallas_call(
        _out_kernel,
        out_shape=jax.ShapeDtypeStruct((nseq, t, D_MODEL), F32),
        grid=(nseq // bs, t // bt),
        in_specs=[row(D_MODEL, 0),
                  pl.BlockSpec((bs, 1, 3 * D_MODEL), lambda i, k: (i, 0, 0)),
                  halves, halves, halves, halves, halves, halves,
                  row(WIDTH_A, REST_ZA // WIDTH_A),
                  row(WIDTH_B, 0),
                  row(WIDTH_B, REST_ZB // WIDTH_B),
                  row(D_MODEL, REST_GA // D_MODEL),
                  row(D_MODEL, REST_GB // D_MODEL),
                  per_layer((WIDTH_A, D_MODEL)), per_layer((WIDTH_B, D_MODEL)), per_layer((D_MODEL, D_MODEL)),
                  per_layer((1, D_MODEL)), per_layer((1, D_MODEL))],
        out_specs=row(D_MODEL, 0),
        compiler_params=_cparams(("arbitrary", "arbitrary")),
        name="output_block",
    )(x, mod, o_g[0], o_g[1], o_g[2], lse_g[0], lse_g[1], lse_g[2], rest, o_b, rest, rest, rest,
      wa, wb, wo, ln_g, ln_b)


def _alibi_slopes():
    i = jnp.arange(1, N_HEADS_A + 1, dtype=F32)
    return jnp.exp2(-8.0 * i / N_HEADS_A).reshape(N_GROUPS, HEADS_PER_GROUP)


def _split_w_in(w_in):
    zeros = lambda n: jnp.zeros(w_in.shape[:2] + (n,), BF16)
    att = jnp.concatenate([w_in[:, :, :SRC_ZA].astype(BF16), w_in[:, :, SRC_AB:SRC_GA].astype(BF16),
                           zeros(LANES - 2 * N_HEADS_B)], axis=-1)
    rest = jnp.concatenate([w_in[:, :, SRC_ZA:SRC_AB].astype(BF16), zeros(REST_GA - (SRC_AB - SRC_ZA)),
                            w_in[:, :, SRC_GA:].astype(BF16)], axis=-1)
    return att, rest


def _split_kv(att, group, start):
    b = att.shape[0]
    k = att[:, (N_GROUPS + group) * HALVES:(N_GROUPS + group + 1) * HALVES, start:, :]
    v = att[:, (2 * N_GROUPS + group) * HALVES:(2 * N_GROUPS + group + 1) * HALVES, start:, :]
    kv = jnp.stack([k, v], axis=1)
    kv = kv.transpose(0, 3, 1, 2, 4)
    return kv.reshape(b, kv.shape[1], 2, HEADS_PER_GROUP, HEAD_DIM_A)


def _feature_major(cache):
    d, b, l = cache.shape[:3]
    return cache.transpose(0, 1, 3, 4, 5, 2).reshape(d, b, 2, GROUP_W, l)


def kernel(x_prompt, x_sample, c_prompt, c_sample, cache_win1, cache_win2, cache_win3, state_gdn, state_conv,
           w_ada, b_ada, w_in, conv_w, a_log, dt_bias, gdn_norm, w_out_a, w_out_b, w_o, ln_g, ln_b):
    n_p, seq, _ = x_prompt.shape
    n_s, t_new, _ = x_sample.shape
    caches = tuple(_feature_major(c) for c in (cache_win1, cache_win2, cache_win3))
    sample_attn_tiling = ((8, 4), (8, 2), (2, 2))
    slopes = _alibi_slopes()

    w_att, w_rest = _split_w_in(w_in)
    wa = (0.5 * w_out_a).astype(BF16)
    wb = (0.5 * w_out_b).astype(BF16)
    wo = w_o.astype(BF16)
    ln_g3 = ln_g.reshape(DEPTH, 1, D_MODEL)
    ln_b3 = ln_b.reshape(DEPTH, 1, D_MODEL)
    conv_w_p = jnp.pad(0.5 * conv_w, ((0, 0), (0, SUBLANES - CONV_WIDTH), (0, 0)))
    params = jnp.zeros((DEPTH, SUBLANES, LANES), F32)
    params = params.at[:, 0, :N_HEADS_B].set(a_log).at[:, 1, :N_HEADS_B].set(dt_bias).at[:, 2, :].set(gdn_norm)

    n_all = n_p + n_s
    n_pad = -(-n_all // SUBLANES) * SUBLANES
    c_all = jnp.pad(jnp.concatenate([c_prompt, c_sample], axis=0), ((0, n_pad - n_all), (0, 0)))
    mod_all = _modulation(c_all, w_ada, b_ada)

    zero_conv = jnp.zeros((n_p, SUBLANES, CONV_DIM), F32)
    zero_state = jnp.zeros((n_p, N_HEADS_B, HEAD_DIM_B, HEAD_DIM_B), F32)
    conv_s = jnp.pad(state_conv, ((0, 0), (0, 0), (SUBLANES - (CONV_WIDTH - 1), 0), (0, 0)))

    xp, xs = x_prompt, x_sample
    acc_p = [[] for _ in range(N_GROUPS + 2)]
    acc_s = [[] for _ in range(N_GROUPS + 2)]
    gdn_sample = None
    for l in range(DEPTH):
        mod_p = mod_all[l, :n_p, None, :]
        mod_s = mod_all[l, n_p:n_all, None, :]

        att_p, rest_p = _in_projection(xp, mod_p, w_att, w_rest, l, 1, PROMPT_PROJ_ROW_TILE, BF16)
        att_s, rest_s = _in_projection(xs, mod_s, w_att, w_rest, l, SAMPLE_SEQ_TILE, t_new, F32)

        o_p, lse_p, o_s, lse_s = [], [], [], []
        for g in range(N_GROUPS):
            o, lse = _attention_prompt(att_p, slopes, g)
            o_p.append(o)
            lse_p.append(lse)
            o, lse = _attention_sample(att_s, caches[g], l, slopes, g, *sample_attn_tiling[g])
            o_s.append(o)
            lse_s.append(lse)

        ob_p, st_p = _gated_deltanet(rest_p, att_p, zero_conv, zero_state, l, conv_w_p, params, PROMPT_GDN_SEQS,
                                     PROMPT_GDN_SEQS, PROMPT_GDN_CHUNK)
        ob_s, gdn_sample = _gated_deltanet(rest_s, att_s, conv_s[l], state_gdn, l, conv_w_p, params,
                                           *SAMPLE_GDN_TILING, t_new, state_buf=gdn_sample)

        xp_new = _output_block(xp, mod_p, o_p, lse_p, ob_p, rest_p, wa, wb, wo, ln_g3, ln_b3, l,
                               1, PROMPT_OUT_ROW_TILE)
        xs_new = _output_block(xs, mod_s, o_s, lse_s, ob_s, rest_s, wa, wb, wo, ln_g3, ln_b3, l,
                               SAMPLE_SEQ_TILE, t_new)

        for g in range(N_GROUPS):
            acc_p[g].append(_split_kv(att_p, g, seq - min(WINDOWS[g], seq)))
            acc_s[g].append(_split_kv(att_s, g, 0))
        acc_p[N_GROUPS].append(st_p)
        acc_p[N_GROUPS + 1].append(rest_p[:, seq - (CONV_WIDTH - 1):, REST_QKVB:REST_QKVB + CONV_DIM].astype(F32))
        acc_s[N_GROUPS + 1].append(rest_s[:, t_new - (CONV_WIDTH - 1):, REST_QKVB:REST_QKVB + CONV_DIM])
        xp, xs = xp_new, xs_new

    acc_s[N_GROUPS] = None
    outs_p = [jnp.stack(a) for a in acc_p]
    outs_s = [gdn_sample if a is None else jnp.stack(a) for a in acc_s]
    return (xp, xs, *outs_p, *outs_s)
```

```python
import functools

import numpy as np
import jax
import jax.numpy as jnp
from jax import lax
from jax.experimental import pallas as pl
from jax.experimental.pallas import tpu as pltpu

F32 = jnp.float32
BF16 = jnp.bfloat16
HIGHEST = lax.Precision.HIGHEST

D_MODEL = 1024
DEPTH = 4
DILATIONS = (1, 4, 16)
WINDOWS = (128, 512, 2048)
N_GROUPS = 3
HEADS_PER_GROUP = 4
HEAD_DIM_A = 64
GROUP_W = HEADS_PER_GROUP * HEAD_DIM_A
WIDTH_A = N_GROUPS * GROUP_W
N_HEADS_A = N_GROUPS * HEADS_PER_GROUP
STEPS_BACK = 128
A_BLOCK = 128
N_HEADS_B = 6
HEAD_DIM_B = 128
WIDTH_B = N_HEADS_B * HEAD_DIM_B
CONV_WIDTH = 4
CONV_DIM = 3 * WIDTH_B
DEEPNORM_ALPHA = (2 * DEPTH) ** 0.25
LN_EPS = 1e-5
RMS_EPS = 1e-6
LANES = 128
SUBLANES = 8
HALVES = GROUP_W // LANES

SRC_ZA = 3 * WIDTH_A
SRC_QKVB = SRC_ZA + WIDTH_A
SRC_ZB = SRC_QKVB + CONV_DIM
SRC_AB = SRC_ZB + WIDTH_B
SRC_GA = SRC_AB + 2 * N_HEADS_B
SRC_GB = SRC_GA + D_MODEL

AB_BLOCK = 3 * WIDTH_A // LANES
ATT_BLOCKS = AB_BLOCK + 1
ATT_W = ATT_BLOCKS * LANES
REST_ZA = 0
REST_QKVB = 768
REST_ZB = 3072
REST_GA = 4096
REST_GB = 5120
REST_W = 6144
REST_TILE = 2048

PROMPT_GDN_CHUNK = 128
PROMPT_GDN_SEQS = 2
TRI_BASE = 16
ATTN_TASKS = 2
ATTN_STATIC_BLOCKS = 4
PROMPT_ROW_TILE = 512
PROMPT_PROJ_ROW_TILE = 1024
PROMPT_OUT_ROW_TILE = 512
SAMPLE_SEQ_TILE = 32
SAMPLE_GDN_TILING = (8, 4)
VMEM_LIMIT = 48 * 1024 * 1024

NT_DIMS = (((1,), (1,)), ((), ()))


def _cparams(sem):
    return pltpu.CompilerParams(dimension_semantics=sem, vmem_limit_bytes=VMEM_LIMIT)


def _sigmoid(x):
    return 0.5 * jnp.tanh(0.5 * x) + 0.5


def _silu_of_half(h):
    return h + h * jnp.tanh(h)


def _silu(x):
    return _silu_of_half(0.5 * x)


def _bdot(a, b):
    return jnp.dot(a.astype(BF16), b.astype(BF16), preferred_element_type=F32)


def _bdot_nt(a, b):
    return lax.dot_general(a.astype(BF16), b.astype(BF16), NT_DIMS, preferred_element_type=F32)


def _mod_kernel(c_ref, w_ref, b_ref, o_ref):
    o_ref[0] = _bdot(_silu(c_ref[...]), w_ref[0]) + b_ref[0]


def _modulation(c_all, w_ada, b_ada):
    n = c_all.shape[0]
    return pl.pallas_call(
        _mod_kernel,
        out_shape=jax.ShapeDtypeStruct((DEPTH, n, 3 * D_MODEL), F32),
        grid=(DEPTH, 3),
        in_specs=[
            pl.BlockSpec((n, D_MODEL), lambda l, j: (0, 0)),
            pl.BlockSpec((1, D_MODEL, D_MODEL), lambda l, j: (l, 0, j)),
            pl.BlockSpec((1, 1, D_MODEL), lambda l, j: (l, 0, j)),
        ],
        out_specs=pl.BlockSpec((1, n, D_MODEL), lambda l, j: (l, 0, j)),
        compiler_params=_cparams(("arbitrary", "arbitrary")),
        name="adaln_modulation",
    )(c_all, w_ada, b_ada.reshape(DEPTH, 1, 3 * D_MODEL))


def _modulated(x_ref, mod_ref):
    bs, bt, _ = x_ref.shape
    shift = mod_ref[:, :, 0:D_MODEL]
    scale = mod_ref[:, :, D_MODEL:2 * D_MODEL]
    h = x_ref[...] * (1.0 + scale) + shift
    return h.reshape(bs * bt, D_MODEL).astype(BF16)


def _inproj_att_kernel(x_ref, mod_ref, w_ref, o_ref):
    bs, bt, _ = x_ref.shape
    res = jnp.dot(_modulated(x_ref, mod_ref), w_ref[0], preferred_element_type=F32)
    for j in range(ATT_BLOCKS):
        o_ref[:, j] = res[:, j * LANES:(j + 1) * LANES].reshape(bs, bt, LANES)


def _inproj_rest_kernel(x_ref, mod_ref, w_ref, scale_ref, o_ref):
    bs, bt, _ = x_ref.shape
    res = jnp.dot(_modulated(x_ref, mod_ref), w_ref[0], preferred_element_type=F32)
    res = res * scale_ref[...]
    o_ref[...] = res.reshape(bs, bt, REST_TILE).astype(o_ref.dtype)


def _rest_column_scale():
    scale = np.full((1, REST_W), 0.5, np.float32)
    scale[:, REST_QKVB:REST_ZB] = 1.0
    return jnp.asarray(scale)


def _in_projection(x, mod, w_att, w_rest, layer, bs, bt, rest_dtype):
    nseq, t, _ = x.shape
    att = pl.pallas_call(
        _inproj_att_kernel,
        out_shape=jax.ShapeDtypeStruct((nseq, ATT_BLOCKS, t, LANES), F32),
        grid=(nseq // bs, t // bt),
        in_specs=[
            pl.BlockSpec((bs, bt, D_MODEL), lambda i, k: (i, k, 0)),
            pl.BlockSpec((bs, 1, 3 * D_MODEL), lambda i, k: (i, 0, 0)),
            pl.BlockSpec((1, D_MODEL, ATT_W), lambda i, k: (layer, 0, 0)),
        ],
        out_specs=pl.BlockSpec((bs, ATT_BLOCKS, bt, LANES), lambda i, k: (i, 0, k, 0)),
        compiler_params=_cparams(("arbitrary", "arbitrary")),
        name="in_projection_att",
    )(x, mod, w_att)
    rest = pl.pallas_call(
        _inproj_rest_kernel,
        out_shape=jax.ShapeDtypeStruct((nseq, t, REST_W), rest_dtype),
        grid=(REST_W // REST_TILE, nseq // bs, t // bt),
        in_specs=[
            pl.BlockSpec((bs, bt, D_MODEL), lambda j, i, k: (i, k, 0)),
            pl.BlockSpec((bs, 1, 3 * D_MODEL), lambda j, i, k: (i, 0, 0)),
            pl.BlockSpec((1, D_MODEL, REST_TILE), lambda j, i, k: (layer, 0, j)),
            pl.BlockSpec((1, REST_TILE), lambda j, i, k: (0, j)),
        ],
        out_specs=pl.BlockSpec((bs, bt, REST_TILE), lambda j, i, k: (i, k, j)),
        compiler_params=_cparams(("arbitrary", "arbitrary", "arbitrary")),
        name="in_projection_rest",
    )(x, mod, w_rest, _rest_column_scale())
    return att, rest


def _attn_prompt_kernel(slopes_ref, q_ref, kp_ref, ko_ref, vp_ref, vo_ref, o_ref, lse_ref, *, group, dilation, units,
                        tasks):
    n = pl.program_id(1)
    span = A_BLOCK * dilation
    qi = lax.broadcasted_iota(jnp.int32, (A_BLOCK, 2 * A_BLOCK), 0)
    kj = lax.broadcasted_iota(jnp.int32, (A_BLOCK, 2 * A_BLOCK), 1)
    steps = A_BLOCK + qi - kj
    band = (steps >= 0) & (steps <= STEPS_BACK)
    dist = (steps * dilation).astype(F32)
    alibi = [jnp.where(band, -slopes_ref[group, h] * dist, -jnp.inf) for h in range(HEADS_PER_GROUP)]
    no_prev = jnp.where((kj >= A_BLOCK) | (n > 0), 0.0, -jnp.inf)
    lane_head = lax.broadcasted_iota(jnp.int32, (1, GROUP_W), 1) // HEAD_DIM_A
    heads = range(HEADS_PER_GROUP)

    def rows_of(ref, rows):
        return jnp.concatenate([ref[0, half, rows, :] for half in range(HALVES)], axis=-1)

    def run(tasks):
        q, k, v, own_rows = [], [], [], []
        for u, r in tasks:
            own = pl.ds(u * span + r, A_BLOCK, stride=dilation)
            if u == 0:
                prev_k, prev_v, prev = kp_ref, vp_ref, pl.ds(r, A_BLOCK, stride=dilation)
            else:
                prev_k, prev_v, prev = ko_ref, vo_ref, pl.ds((u - 1) * span + r, A_BLOCK, stride=dilation)
            own_rows.append(own)
            q.append(rows_of(q_ref, own) * (HEAD_DIM_A ** -0.5))
            k.append(jnp.concatenate([rows_of(prev_k, prev), rows_of(ko_ref, own)], axis=0).astype(BF16))
            v.append(jnp.concatenate([rows_of(prev_v, prev), rows_of(vo_ref, own)], axis=0).astype(BF16))
        chains = [(t, h) for t in range(len(tasks)) for h in heads]
        s = [lax.dot_general(jnp.where(lane_head == h, q[t], 0.0).astype(BF16), k[t], NT_DIMS,
                             preferred_element_type=F32) for t, h in chains]
        s = [s[c] + alibi[h] for c, (t, h) in enumerate(chains)]
        s = [s[c] + no_prev if tasks[t][0] == 0 else s[c] for c, (t, h) in enumerate(chains)]
        m = [x.max(-1, keepdims=True) for x in s]
        p = [jnp.exp(x - mx) for x, mx in zip(s, m)]
        den = [x.sum(-1, keepdims=True) for x in p]
        oh = [jnp.dot((p[c] / den[c]).astype(BF16), v[t], preferred_element_type=F32) for c, (t, h) in enumerate(chains)]
        for t in range(len(tasks)):
            o_acc = jnp.zeros((A_BLOCK, GROUP_W), F32)
            lse_acc = jnp.zeros((A_BLOCK, GROUP_W), F32)
            for h in heads:
                c = t * HEADS_PER_GROUP + h
                o_acc = jnp.where(lane_head == h, oh[c], o_acc)
                lse_acc = jnp.where(lane_head == h, m[c] + jnp.log(den[c]), lse_acc)
            for half in range(HALVES):
                o_ref[0, half, own_rows[t], :] = o_acc[:, half * LANES:(half + 1) * LANES]
                lse_ref[0, half, own_rows[t], :] = lse_acc[:, half * LANES:(half + 1) * LANES]

    if dilation * units <= ATTN_STATIC_BLOCKS:
        blocks = [(u, r) for u in range(units) for r in range(dilation)]
        for j in range(0, len(blocks), tasks):
            run(blocks[j:j + tasks])
    else:
        for u in range(units):
            def residues(j, carry, u=u):
                run([(u, j * tasks + t) for t in range(tasks)])
                return carry
            lax.fori_loop(0, dilation // tasks, residues, 0)


def _attention_prompt(att, slopes, group):
    b, _, s, _ = att.shape
    d = DILATIONS[group]
    span = A_BLOCK * d
    units = max(1, PROMPT_ROW_TILE // span)
    rows = span * units
    tasks = ATTN_TASKS * (2 if d > SUBLANES else 1)
    qb, kb, vb = group, N_GROUPS + group, 2 * N_GROUPS + group
    own = lambda col: pl.BlockSpec((1, HALVES, rows, LANES), lambda i, n: (i, col, n, 0))
    prev = lambda col: pl.BlockSpec((1, HALVES, span, LANES), lambda i, n: (i, col, jnp.maximum(n * units - 1, 0), 0))
    out_spec = pl.BlockSpec((1, HALVES, rows, LANES), lambda i, n: (i, 0, n, 0))
    return pl.pallas_call(
        functools.partial(_attn_prompt_kernel, group=group, dilation=d, units=units, tasks=tasks),
        out_shape=[jax.ShapeDtypeStruct((b, HALVES, s, LANES), F32)] * 2,
        grid=(b, s // rows),
        in_specs=[pl.BlockSpec(memory_space=pltpu.SMEM), own(qb), prev(kb), own(kb), prev(vb), own(vb)],
        out_specs=[out_spec, out_spec],
        compiler_params=_cparams(("arbitrary", "arbitrary")),
        name=f"attn_prompt_g{group}",
    )(slopes, att, att, att, att, att)


def _sample_bias(slopes, group, t_new):
    d = DILATIONS[group]
    lb = WINDOWS[group]
    t = np.arange(t_new)

    def table(key_pos, width):
        diff = (lb + t)[:, None] - key_pos[None, :]
        ok = (diff >= 0) & (diff % d == 0) & (diff // d <= STEPS_BACK)
        dist = jnp.asarray(np.where(ok, diff, 0), F32)
        bias = jnp.where(jnp.asarray(ok)[None], -slopes[group][:, None, None] * dist[None], -jnp.inf)
        bias = bias.reshape(HEADS_PER_GROUP * t_new, key_pos.shape[0])
        return jnp.pad(bias, ((0, 0), (0, width - key_pos.shape[0])), constant_values=-jnp.inf)

    return table(np.arange(lb), lb), table(lb + t, LANES)


def _attn_sample_kernel(q_ref, k_ref, v_ref, cache_ref, bias_ref, biasn_ref, o_ref, lse_ref, *, bb, per_iter, t_new):
    nq = HEADS_PER_GROUP * t_new
    row_head = lax.broadcasted_iota(jnp.int32, (nq, GROUP_W), 0) // t_new
    lane_head2 = lax.broadcasted_iota(jnp.int32, (nq, GROUP_W), 1) // HEAD_DIM_A
    lane_head = lax.broadcasted_iota(jnp.int32, (1, GROUP_W), 1) // HEAD_DIM_A
    pad_n = jnp.zeros((LANES - t_new, GROUP_W), F32)

    def new_rows(ref, i):
        return jnp.concatenate([ref[i, half] for half in range(HALVES)], axis=-1)

    def body(it, carry):
        seqs = [it * per_iter + s for s in range(per_iter)]
        each = lambda f: [f(i) for i in seqs]
        n = range(per_iter)
        q = each(lambda i: new_rows(q_ref, i) * (HEAD_DIM_A ** -0.5))
        qm = [jnp.where(row_head == lane_head2, jnp.concatenate([x] * HEADS_PER_GROUP, axis=0), 0.0).astype(BF16)
              for x in q]
        kt = each(lambda i: cache_ref[0, i, 0].astype(BF16))
        vt = each(lambda i: cache_ref[0, i, 1].astype(BF16))
        kn = each(lambda i: jnp.concatenate([new_rows(k_ref, i), pad_n], axis=0).astype(BF16))
        vn = each(lambda i: jnp.concatenate([new_rows(v_ref, i), pad_n], axis=0).astype(BF16))
        sc = [jnp.dot(qm[c], kt[c], preferred_element_type=F32) + bias_ref[...] for c in n]
        sn = [lax.dot_general(qm[c], kn[c], NT_DIMS, preferred_element_type=F32) + biasn_ref[...] for c in n]
        m = [jnp.maximum(sc[c].max(-1, keepdims=True), sn[c].max(-1, keepdims=True)) for c in n]
        pc = [jnp.exp(sc[c] - m[c]) for c in n]
        pn = [jnp.exp(sn[c] - m[c]) for c in n]
        den = [pc[c].sum(-1, keepdims=True) + pn[c].sum(-1, keepdims=True) for c in n]
        of = [lax.dot_general((pc[c] / den[c]).astype(BF16), vt[c], NT_DIMS, preferred_element_type=F32)
              + jnp.dot((pn[c] / den[c]).astype(BF16), vn[c], preferred_element_type=F32) for c in n]
        for c, i in enumerate(seqs):
            lse_col = m[c] + jnp.log(den[c])
            o = jnp.zeros((t_new, GROUP_W), F32)
            lse = jnp.zeros((t_new, GROUP_W), F32)
            for h in range(HEADS_PER_GROUP):
                head = lane_head == h
                o = jnp.where(head, of[c][h * t_new:(h + 1) * t_new, :], o)
                lse = jnp.where(head, lse_col[h * t_new:(h + 1) * t_new, :], lse)
            for half in range(HALVES):
                o_ref[i, half] = o[:, half * LANES:(half + 1) * LANES]
                lse_ref[i, half] = lse[:, half * LANES:(half + 1) * LANES]
        return carry

    lax.fori_loop(0, bb // per_iter, body, 0)


def _attention_sample(att, cache_t, layer, slopes, group, bb, per_iter):
    nseq, _, t_new, _ = att.shape
    lb = WINDOWS[group]
    bias, bias_new = _sample_bias(slopes, group, t_new)
    qb, kb, vb = group, N_GROUPS + group, 2 * N_GROUPS + group
    new_spec = lambda col: pl.BlockSpec((bb, HALVES, t_new, LANES), lambda i: (i, col, 0, 0))
    out_spec = pl.BlockSpec((bb, HALVES, t_new, LANES), lambda i: (i, 0, 0, 0))
    return pl.pallas_call(
        functools.partial(_attn_sample_kernel, bb=bb, per_iter=per_iter, t_new=t_new),
        out_shape=[jax.ShapeDtypeStruct((nseq, HALVES, t_new, LANES), F32)] * 2,
        grid=(nseq // bb,),
        in_specs=[new_spec(qb), new_spec(kb), new_spec(vb),
                  pl.BlockSpec((1, bb, 2, GROUP_W, lb), lambda i: (layer, i, 0, 0, 0)),
                  pl.BlockSpec(bias.shape, lambda i: (0, 0)), pl.BlockSpec(bias_new.shape, lambda i: (0, 0))],
        out_specs=[out_spec, out_spec],
        compiler_params=_cparams(("arbitrary",)),
        name=f"attn_sample_g{group}",
    )(att, att, att, cache_t, bias, bias_new)


def _unit_lower_inverses(lowers, ri, ci):
    c = lowers[0].shape[0]
    base = min(TRI_BASE, c)
    ident = (ri == ci).astype(F32)
    same = (ri // base) == (ci // base)
    powers = [jnp.where(same, l, 0.0) for l in lowers]
    invs = [ident - p for p in powers]
    for _ in range(int(np.log2(base)) - 1):
        powers = [_bdot(p, p) for p in powers]
        invs = [t + _bdot(t, p) for t, p in zip(invs, powers)]
    size = base
    while size < c:
        pair = ((ri // (2 * size)) == (ci // (2 * size))) & ((ri // size) % 2 == 1) & ((ci // size) % 2 == 0)
        offs = [_bdot(jnp.where(pair, l, 0.0), t) for l, t in zip(lowers, invs)]
        invs = [t - _bdot(t, o) for t, o in zip(invs, offs)]
        size *= 2
    return invs


def _gdn_kernel(*refs, bb, per_iter, chunk, layered, single_chunk, aliased):
    (q_ref, k_ref, v_ref, ab_ref, conv0_ref, s0_ref, cw_ref, par_ref), refs = refs[:8], refs[8 + int(aliased):]
    o_ref, s_ref, ext_ref = refs
    c_idx = pl.program_id(1)
    C = chunk
    first_row = SUBLANES - (CONV_WIDTH - 1)
    if layered:
        s0_ref, s_ref = s0_ref.at[0], s_ref.at[0]
    cw_ref, par_ref = cw_ref.at[0], par_ref.at[0]
    s_prev = s0_ref if single_chunk else s_ref

    @pl.when(c_idx == 0)
    def _():
        ext_ref[:, 0:SUBLANES, :] = conv0_ref[...]
        if not single_chunk:
            s_ref[...] = s0_ref[...]

    @pl.when(c_idx > 0)
    def _():
        ext_ref[:, 0:SUBLANES, :] = ext_ref[:, C:C + SUBLANES, :]

    ext_ref[:, SUBLANES:SUBLANES + C, 0:WIDTH_B] = q_ref[...].astype(F32)
    ext_ref[:, SUBLANES:SUBLANES + C, WIDTH_B:2 * WIDTH_B] = k_ref[...].astype(F32)
    ext_ref[:, SUBLANES:SUBLANES + C, 2 * WIDTH_B:3 * WIDTH_B] = v_ref[...].astype(F32)

    ri = lax.broadcasted_iota(jnp.int32, (C, C), 0)
    ci = lax.broadcasted_iota(jnp.int32, (C, C), 1)
    causal = ri >= ci
    strict = ri > ci
    tril = causal.astype(F32)
    neg_a = -jnp.exp(par_ref[0:1, :])
    dt_bias = par_ref[1:2, :]
    norm_w = par_ref[2:3, :]
    pad_rows = jnp.zeros((HEAD_DIM_B - C, HEAD_DIM_B), F32) if C < HEAD_DIM_B else None

    def pad128(x):
        return x if pad_rows is None else jnp.concatenate([x, pad_rows], axis=0)

    def conv(i, col):
        acc = ext_ref[i, pl.ds(first_row, C), pl.ds(col, HEAD_DIM_B)] * cw_ref[0:1, pl.ds(col, HEAD_DIM_B)]
        for j in range(1, CONV_WIDTH):
            acc = acc + (ext_ref[i, pl.ds(first_row + j, C), pl.ds(col, HEAD_DIM_B)]
                         * cw_ref[j:j + 1, pl.ds(col, HEAD_DIM_B)])
        return _silu_of_half(acc)

    def l2n(x, scale=1.0):
        return x * (lax.rsqrt(jnp.sum(x * x, axis=-1, keepdims=True) + RMS_EPS) * scale)

    def gates(i):
        ab = ab_ref[i, 0]
        z = ab + dt_bias
        softplus = jnp.maximum(z, 0.0) + jnp.log(1.0 + jnp.exp(-jnp.abs(z)))
        g_all = neg_a * softplus
        beta_all = _sigmoid(ab)
        gcum_all = jnp.dot(tril, g_all, preferred_element_type=F32, precision=HIGHEST)
        return beta_all, gcum_all, pad128(gcum_all).T

    def body(it, carry):
        seqs = [it * per_iter + s for s in range(per_iter)]
        beta_all, gcum_all, gcum_rows = zip(*[gates(i) for i in seqs])
        chains = [(s, h) for s in range(per_iter) for h in range(N_HEADS_B)]
        each = lambda f: [f(s, h) for s, h in chains]
        n = range(len(chains))
        states = each(lambda s, h: s_prev[seqs[s], h])
        q = each(lambda s, h: l2n(conv(seqs[s], h * HEAD_DIM_B), HEAD_DIM_B ** -0.5))
        k = each(lambda s, h: l2n(conv(seqs[s], WIDTH_B + h * HEAD_DIM_B)))
        v = each(lambda s, h: conv(seqs[s], 2 * WIDTH_B + h * HEAD_DIM_B))
        beta = each(lambda s, h: beta_all[s][:, N_HEADS_B + h:N_HEADS_B + h + 1])
        gcum = each(lambda s, h: gcum_all[s][:, h:h + 1])
        grow = each(lambda s, h: gcum_rows[s][h:h + 1, 0:C])
        decay = [jnp.exp(jnp.where(causal, gcum[c] - grow[c], -jnp.inf)) for c in n]
        qk = [_bdot_nt(jnp.concatenate([q[c], k[c]], axis=0), k[c]) for c in n]
        intra = [qk[c][0:C] * decay[c] for c in n]
        lower = [jnp.where(strict, qk[c][C:2 * C] * beta[c] * decay[c], 0.0) for c in n]
        inv = _unit_lower_inverses(lower, ri, ci)
        e_g = [jnp.exp(gcum[c]) for c in n]
        uw = [_bdot(inv[c], jnp.concatenate([v[c] * beta[c], k[c] * (beta[c] * e_g[c])], axis=-1)) for c in n]
        ws_qs = [_bdot(jnp.concatenate([uw[c][:, HEAD_DIM_B:2 * HEAD_DIM_B], q[c] * e_g[c]], axis=0), states[c])
                 for c in n]
        v_new = [uw[c][:, 0:HEAD_DIM_B] - ws_qs[c][0:C] for c in n]
        g_last = [gcum[c][C - 1:C, :] for c in n]
        k_dec = [k[c] * jnp.exp(g_last[c] - gcum[c]) for c in n]
        o = [ws_qs[c][C:2 * C] + _bdot(intra[c], v_new[c]) for c in n]
        new_states = [states[c] * jnp.exp(g_last[c]) + _bdot(pad128(k_dec[c]).T, pad128(v_new[c])) for c in n]
        for c, (s, h) in enumerate(chains):
            s_ref[seqs[s], h] = new_states[c]
            o_c = o[c] * lax.rsqrt(jnp.mean(o[c] * o[c], axis=-1, keepdims=True) + RMS_EPS) * norm_w
            o_ref[seqs[s], :, pl.ds(h * HEAD_DIM_B, HEAD_DIM_B)] = o_c
        return carry

    lax.fori_loop(0, bb // per_iter, body, 0)


def _gated_deltanet(rest, att, conv0, s0, layer, conv_w, params, bb, per_iter, chunk, state_buf=None):
    nseq, t, _ = rest.shape
    qb = REST_QKVB // WIDTH_B
    col_spec = lambda col: pl.BlockSpec((bb, chunk, WIDTH_B), lambda i, c: (i, c, col))
    state_shape = (bb, N_HEADS_B, HEAD_DIM_B, HEAD_DIM_B)
    layered = s0.ndim == 5
    if layered:
        state_spec = pl.BlockSpec((1,) + state_shape, lambda i, c: (layer, i, 0, 0, 0))
        state_out = jax.ShapeDtypeStruct(s0.shape, F32)
    else:
        state_spec = pl.BlockSpec(state_shape, lambda i, c: (i, 0, 0, 0))
        state_out = jax.ShapeDtypeStruct((nseq,) + state_shape[1:], F32)
    in_specs = [col_spec(qb), col_spec(qb + 1), col_spec(qb + 2),
                pl.BlockSpec((bb, 1, chunk, LANES), lambda i, c: (i, AB_BLOCK, c, 0)),
                pl.BlockSpec((bb, SUBLANES, CONV_DIM), lambda i, c: (i, 0, 0)),
                state_spec,
                pl.BlockSpec((1, SUBLANES, CONV_DIM), lambda i, c: (layer, 0, 0)),
                pl.BlockSpec((1, SUBLANES, LANES), lambda i, c: (layer, 0, 0))]
    args = [rest, rest, rest, att, conv0, s0, conv_w, params]
    aliases = {}
    if state_buf is not None:
        aliases = {len(args): 1}
        in_specs.append(pl.BlockSpec(memory_space=pl.ANY))
        args.append(state_buf)
    return pl.pallas_call(
        functools.partial(_gdn_kernel, bb=bb, per_iter=per_iter, chunk=chunk, layered=layered,
                          single_chunk=(t == chunk), aliased=state_buf is not None),
        out_shape=[jax.ShapeDtypeStruct((nseq, t, WIDTH_B), F32), state_out],
        grid=(nseq // bb, t // chunk),
        in_specs=in_specs,
        out_specs=[pl.BlockSpec((bb, chunk, WIDTH_B), lambda i, c: (i, c, 0)), state_spec],
        scratch_shapes=[pltpu.VMEM((bb, chunk + SUBLANES, CONV_DIM), F32)],
        input_output_aliases=aliases,
        compiler_params=_cparams(("arbitrary", "arbitrary")),
        name="gated_deltanet",
    )(*args)


def _out_kernel(x_ref, mod_ref, o1_ref, o2_ref, o3_ref, l1_ref, l2_ref, l3_ref, za_ref, ob_ref, zb_ref, ga_ref, gb_ref,
                wa_ref, wb_ref, wo_ref, lng_ref, lnb_ref, y_ref):
    bs, bt, _ = x_ref.shape
    rows = bs * bt
    flat = lambda ref: ref[...].reshape(rows, ref.shape[-1]).astype(F32)
    group = lambda ref: jnp.concatenate([ref[:, half].reshape(rows, LANES) for half in range(HALVES)], axis=-1)
    l1, l2, l3 = group(l1_ref), group(l2_ref), group(l3_ref)
    m = jnp.maximum(jnp.maximum(l1, l2), l3)
    e1, e2, e3 = jnp.exp(l1 - m), jnp.exp(l2 - m), jnp.exp(l3 - m)
    inv_tot = 1.0 / (e1 + e2 + e3)
    att = jnp.concatenate([group(o1_ref) * (e1 * inv_tot), group(o2_ref) * (e2 * inv_tot),
                           group(o3_ref) * (e3 * inv_tot)], axis=-1)
    half_a = jnp.dot((att * _silu_of_half(flat(za_ref))).astype(BF16), wa_ref[0], preferred_element_type=F32)
    half_b = jnp.dot((flat(ob_ref) * _silu_of_half(flat(zb_ref))).astype(BF16), wb_ref[0], preferred_element_type=F32)
    merged = (half_a + half_a * jnp.tanh(flat(ga_ref))) + (half_b + half_b * jnp.tanh(flat(gb_ref)))
    out = jnp.dot(merged.astype(BF16), wo_ref[0], preferred_element_type=F32).reshape(bs, bt, D_MODEL)
    gate = mod_ref[:, :, 2 * D_MODEL:3 * D_MODEL]
    y = DEEPNORM_ALPHA * x_ref[...] + gate * out
    mu = jnp.mean(y, axis=-1, keepdims=True)
    yc = y - mu
    var = jnp.mean(yc * yc, axis=-1, keepdims=True)
    y_ref[...] = yc * lax.rsqrt(var + LN_EPS) * lng_ref[...] + lnb_ref[...]


def _output_block(x, mod, o_g, lse_g, o_b, rest, wa, wb, wo, ln_g, ln_b, layer, bs, bt):
    nseq, t, _ = x.shape
    row = lambda w, col: pl.BlockSpec((bs, bt, w), lambda i, k: (i, k, col))
    halves = pl.BlockSpec((bs, HALVES, bt, LANES), lambda i, k: (i, 0, k, 0))
    per_layer = lambda shape: pl.BlockSpec((1,) + shape, lambda i, k: (layer,) + (0,) * len(shape))
    return pl.pallas_call(
        _out_kernel,
        out_shape=jax.ShapeDtypeStruct((nseq, t, D_MODEL), F32),
        grid=(nseq // bs, t // bt),
        in_specs=[row(D_MODEL, 0),
                  pl.BlockSpec((bs, 1, 3 * D_MODEL), lambda i, k: (i, 0, 0)),
                  halves, halves, halves, halves, halves, halves,
                  row(WIDTH_A, REST_ZA // WIDTH_A),
                  row(WIDTH_B, 0),
                  row(WIDTH_B, REST_ZB // WIDTH_B),
                  row(D_MODEL, REST_GA // D_MODEL),
                  row(D_MODEL, REST_GB // D_MODEL),
                  per_layer((WIDTH_A, D_MODEL)), per_layer((WIDTH_B, D_MODEL)), per_layer((D_MODEL, D_MODEL)),
                  per_layer((1, D_MODEL)), per_layer((1, D_MODEL))],
        out_specs=row(D_MODEL, 0),
        compiler_params=_cparams(("arbitrary", "arbitrary")),
        name="output_block",
    )(x, mod, o_g[0], o_g[1], o_g[2], lse_g[0], lse_g[1], lse_g[2], rest, o_b, rest, rest, rest,
      wa, wb, wo, ln_g, ln_b)


def _alibi_slopes():
    i = jnp.arange(1, N_HEADS_A + 1, dtype=F32)
    return jnp.exp2(-8.0 * i / N_HEADS_A).reshape(N_GROUPS, HEADS_PER_GROUP)


def _split_w_in(w_in):
    zeros = lambda n: jnp.zeros(w_in.shape[:2] + (n,), BF16)
    att = jnp.concatenate([w_in[:, :, :SRC_ZA].astype(BF16), w_in[:, :, SRC_AB:SRC_GA].astype(BF16),
                           zeros(LANES - 2 * N_HEADS_B)], axis=-1)
    rest = jnp.concatenate([w_in[:, :, SRC_ZA:SRC_AB].astype(BF16), zeros(REST_GA - (SRC_AB - SRC_ZA)),
                            w_in[:, :, SRC_GA:].astype(BF16)], axis=-1)
    return att, rest


def _split_kv(att, group, start):
    b = att.shape[0]
    k = att[:, (N_GROUPS + group) * HALVES:(N_GROUPS + group + 1) * HALVES, start:, :]
    v = att[:, (2 * N_GROUPS + group) * HALVES:(2 * N_GROUPS + group + 1) * HALVES, start:, :]
    kv = jnp.stack([k, v], axis=1)
    kv = kv.transpose(0, 3, 1, 2, 4)
    return kv.reshape(b, kv.shape[1], 2, HEADS_PER_GROUP, HEAD_DIM_A)


def _feature_major(cache):
    d, b, l = cache.shape[:3]
    return cache.transpose(0, 1, 3, 4, 5, 2).reshape(d, b, 2, GROUP_W, l)


def kernel(x_prompt, x_sample, c_prompt, c_sample, cache_win1, cache_win2, cache_win3, state_gdn, state_conv,
           w_ada, b_ada, w_in, conv_w, a_log, dt_bias, gdn_norm, w_out_a, w_out_b, w_o, ln_g, ln_b):
    n_p, seq, _ = x_prompt.shape
    n_s, t_new, _ = x_sample.shape
    caches = tuple(_feature_major(c) for c in (cache_win1, cache_win2, cache_win3))
    sample_attn_tiling = ((16, 4), (16, 2), (4, 2))
    slopes = _alibi_slopes()

    w_att, w_rest = _split_w_in(w_in)
    wa = (0.5 * w_out_a).astype(BF16)
    wb = (0.5 * w_out_b).astype(BF16)
    wo = w_o.astype(BF16)
    ln_g3 = ln_g.reshape(DEPTH, 1, D_MODEL)
    ln_b3 = ln_b.reshape(DEPTH, 1, D_MODEL)
    conv_w_p = jnp.pad(0.5 * conv_w, ((0, 0), (0, SUBLANES - CONV_WIDTH), (0, 0)))
    params = jnp.zeros((DEPTH, SUBLANES, LANES), F32)
    params = params.at[:, 0, :N_HEADS_B].set(a_log).at[:, 1, :N_HEADS_B].set(dt_bias).at[:, 2, :].set(gdn_norm)

    n_all = n_p + n_s
    n_pad = -(-n_all // SUBLANES) * SUBLANES
    c_all = jnp.pad(jnp.concatenate([c_prompt, c_sample], axis=0), ((0, n_pad - n_all), (0, 0)))
    mod_all = _modulation(c_all, w_ada, b_ada)

    zero_conv = jnp.zeros((n_p, SUBLANES, CONV_DIM), F32)
    zero_state = jnp.zeros((n_p, N_HEADS_B, HEAD_DIM_B, HEAD_DIM_B), F32)
    conv_s = jnp.pad(state_conv, ((0, 0), (0, 0), (SUBLANES - (CONV_WIDTH - 1), 0), (0, 0)))

    xp, xs = x_prompt, x_sample
    acc_p = [[] for _ in range(N_GROUPS + 2)]
    acc_s = [[] for _ in range(N_GROUPS + 2)]
    gdn_sample = None
    for l in range(DEPTH):
        mod_p = mod_all[l, :n_p, None, :]
        mod_s = mod_all[l, n_p:n_all, None, :]

        att_p, rest_p = _in_projection(xp, mod_p, w_att, w_rest, l, 1, PROMPT_PROJ_ROW_TILE, BF16)
        att_s, rest_s = _in_projection(xs, mod_s, w_att, w_rest, l, SAMPLE_SEQ_TILE, t_new, F32)

        o_p, lse_p, o_s, lse_s = [], [], [], []
        for g in range(N_GROUPS):
            o, lse = _attention_prompt(att_p, slopes, g)
            o_p.append(o)
            lse_p.append(lse)
            o, lse = _attention_sample(att_s, caches[g], l, slopes, g, *sample_attn_tiling[g])
            o_s.append(o)
            lse_s.append(lse)

        ob_p, st_p = _gated_deltanet(rest_p, att_p, zero_conv, zero_state, l, conv_w_p, params, n_p,
                                     PROMPT_GDN_SEQS, PROMPT_GDN_CHUNK)
        ob_s, gdn_sample = _gated_deltanet(rest_s, att_s, conv_s[l], state_gdn, l, conv_w_p, params,
                                           *SAMPLE_GDN_TILING, t_new, state_buf=gdn_sample)

        xp_new = _output_block(xp, mod_p, o_p, lse_p, ob_p, rest_p, wa, wb, wo, ln_g3, ln_b3, l,
                               1, PROMPT_OUT_ROW_TILE)
        xs_new = _output_block(xs, mod_s, o_s, lse_s, ob_s, rest_s, wa, wb, wo, ln_g3, ln_b3, l,
                               SAMPLE_SEQ_TILE, t_new)

        for g in range(N_GROUPS):
            acc_p[g].append(_split_kv(att_p, g, seq - min(WINDOWS[g], seq)))
            acc_s[g].append(_split_kv(att_s, g, 0))
        acc_p[N_GROUPS].append(st_p)
        acc_p[N_GROUPS + 1].append(rest_p[:, seq - (CONV_WIDTH - 1):, REST_QKVB:REST_QKVB + CONV_DIM].astype(F32))
        acc_s[N_GROUPS + 1].append(rest_s[:, t_new - (CONV_WIDTH - 1):, REST_QKVB:REST_QKVB + CONV_DIM])
        xp, xs = xp_new, xs_new

    acc_s[N_GROUPS] = None
    outs_p = [jnp.stack(a) for a in acc_p]
    outs_s = [gdn_sample if a is None else jnp.stack(a) for a in acc_s]
    return (xp, xs, *outs_p, *outs_s)
```

```python
import functools

import numpy as np
import jax
import jax.numpy as jnp
from jax import lax
from jax.experimental import pallas as pl
from jax.experimental.pallas import tpu as pltpu

F32 = jnp.float32
BF16 = jnp.bfloat16
HIGHEST = lax.Precision.HIGHEST

D_MODEL = 1024
DEPTH = 4
DILATIONS = (1, 4, 16)
WINDOWS = (128, 512, 2048)
N_GROUPS = 3
HEADS_PER_GROUP = 4
HEAD_DIM_A = 64
GROUP_W = HEADS_PER_GROUP * HEAD_DIM_A
WIDTH_A = N_GROUPS * GROUP_W
N_HEADS_A = N_GROUPS * HEADS_PER_GROUP
STEPS_BACK = 128
A_BLOCK = 128
N_HEADS_B = 6
HEAD_DIM_B = 128
WIDTH_B = N_HEADS_B * HEAD_DIM_B
CONV_WIDTH = 4
CONV_DIM = 3 * WIDTH_B
DEEPNORM_ALPHA = (2 * DEPTH) ** 0.25
LN_EPS = 1e-5
RMS_EPS = 1e-6
LANES = 128
SUBLANES = 8
HALVES = GROUP_W // LANES

SRC_ZA = 3 * WIDTH_A
SRC_QKVB = SRC_ZA + WIDTH_A
SRC_ZB = SRC_QKVB + CONV_DIM
SRC_AB = SRC_ZB + WIDTH_B
SRC_GA = SRC_AB + 2 * N_HEADS_B
SRC_GB = SRC_GA + D_MODEL

AB_BLOCK = 3 * WIDTH_A // LANES
ATT_BLOCKS = AB_BLOCK + 1
ATT_W = ATT_BLOCKS * LANES
REST_ZA = 0
REST_QKVB = 768
REST_ZB = 3072
REST_GA = 4096
REST_GB = 5120
REST_W = 6144
REST_TILE = 2048

PROMPT_GDN_CHUNK = 128
PROMPT_GDN_SEQS = 2
TRI_BASE = 16
ATTN_TASKS = 2
ATTN_STATIC_BLOCKS = 4
PROMPT_ROW_TILE = 512
PROMPT_PROJ_ROW_TILE = 1024
PROMPT_OUT_ROW_TILE = 512
SAMPLE_SEQ_TILE = 32
SAMPLE_GDN_TILING = (8, 4)
VMEM_LIMIT = 48 * 1024 * 1024

NT_DIMS = (((1,), (1,)), ((), ()))


def _cparams(sem):
    return pltpu.CompilerParams(dimension_semantics=sem, vmem_limit_bytes=VMEM_LIMIT)


def _sigmoid(x):
    return 0.5 * jnp.tanh(0.5 * x) + 0.5


def _silu_of_half(h):
    return h + h * jnp.tanh(h)


def _silu(x):
    return _silu_of_half(0.5 * x)


def _bdot(a, b):
    return jnp.dot(a.astype(BF16), b.astype(BF16), preferred_element_type=F32)


def _bdot_nt(a, b):
    return lax.dot_general(a.astype(BF16), b.astype(BF16), NT_DIMS, preferred_element_type=F32)


def _mod_kernel(c_ref, w_ref, b_ref, o_ref):
    o_ref[0] = _bdot(_silu(c_ref[...]), w_ref[0]) + b_ref[0]


def _modulation(c_all, w_ada, b_ada):
    n = c_all.shape[0]
    return pl.pallas_call(
        _mod_kernel,
        out_shape=jax.ShapeDtypeStruct((DEPTH, n, 3 * D_MODEL), F32),
        grid=(DEPTH, 3),
        in_specs=[
            pl.BlockSpec((n, D_MODEL), lambda l, j: (0, 0)),
            pl.BlockSpec((1, D_MODEL, D_MODEL), lambda l, j: (l, 0, j)),
            pl.BlockSpec((1, 1, D_MODEL), lambda l, j: (l, 0, j)),
        ],
        out_specs=pl.BlockSpec((1, n, D_MODEL), lambda l, j: (l, 0, j)),
        compiler_params=_cparams(("arbitrary", "arbitrary")),
        name="adaln_modulation",
    )(c_all, w_ada, b_ada.reshape(DEPTH, 1, 3 * D_MODEL))


def _modulated(x_ref, mod_ref):
    bs, bt, _ = x_ref.shape
    shift = mod_ref[:, :, 0:D_MODEL]
    scale = mod_ref[:, :, D_MODEL:2 * D_MODEL]
    h = x_ref[...] * (1.0 + scale) + shift
    return h.reshape(bs * bt, D_MODEL).astype(BF16)


def _inproj_att_kernel(x_ref, mod_ref, w_ref, o_ref):
    bs, bt, _ = x_ref.shape
    res = jnp.dot(_modulated(x_ref, mod_ref), w_ref[0], preferred_element_type=F32)
    for j in range(ATT_BLOCKS):
        o_ref[:, j] = res[:, j * LANES:(j + 1) * LANES].reshape(bs, bt, LANES)


def _inproj_rest_kernel(x_ref, mod_ref, w_ref, scale_ref, o_ref):
    bs, bt, _ = x_ref.shape
    res = jnp.dot(_modulated(x_ref, mod_ref), w_ref[0], preferred_element_type=F32)
    res = res * scale_ref[...]
    o_ref[...] = res.reshape(bs, bt, REST_TILE).astype(o_ref.dtype)


def _rest_column_scale():
    scale = np.full((1, REST_W), 0.5, np.float32)
    scale[:, REST_QKVB:REST_ZB] = 1.0
    return jnp.asarray(scale)


def _in_projection(x, mod, w_att, w_rest, layer, bs, bt, rest_dtype):
    nseq, t, _ = x.shape
    att = pl.pallas_call(
        _inproj_att_kernel,
        out_shape=jax.ShapeDtypeStruct((nseq, ATT_BLOCKS, t, LANES), F32),
        grid=(nseq // bs, t // bt),
        in_specs=[
            pl.BlockSpec((bs, bt, D_MODEL), lambda i, k: (i, k, 0)),
            pl.BlockSpec((bs, 1, 3 * D_MODEL), lambda i, k: (i, 0, 0)),
            pl.BlockSpec((1, D_MODEL, ATT_W), lambda i, k: (layer, 0, 0)),
        ],
        out_specs=pl.BlockSpec((bs, ATT_BLOCKS, bt, LANES), lambda i, k: (i, 0, k, 0)),
        compiler_params=_cparams(("arbitrary", "arbitrary")),
        name="in_projection_att",
    )(x, mod, w_att)
    rest = pl.pallas_call(
        _inproj_rest_kernel,
        out_shape=jax.ShapeDtypeStruct((nseq, t, REST_W), rest_dtype),
        grid=(REST_W // REST_TILE, nseq // bs, t // bt),
        in_specs=[
            pl.BlockSpec((bs, bt, D_MODEL), lambda j, i, k: (i, k, 0)),
            pl.BlockSpec((bs, 1, 3 * D_MODEL), lambda j, i, k: (i, 0, 0)),
            pl.BlockSpec((1, D_MODEL, REST_TILE), lambda j, i, k: (layer, 0, j)),
            pl.BlockSpec((1, REST_TILE), lambda j, i, k: (0, j)),
        ],
        out_specs=pl.BlockSpec((bs, bt, REST_TILE), lambda j, i, k: (i, k, j)),
        compiler_params=_cparams(("arbitrary", "arbitrary", "arbitrary")),
        name="in_projection_rest",
    )(x, mod, w_rest, _rest_column_scale())
    return att, rest


def _attn_prompt_kernel(slopes_ref, q_ref, kp_ref, ko_ref, vp_ref, vo_ref, o_ref, lse_ref, *, group, dilation, units,
                        tasks):
    n = pl.program_id(1)
    span = A_BLOCK * dilation
    qi = lax.broadcasted_iota(jnp.int32, (A_BLOCK, 2 * A_BLOCK), 0)
    kj = lax.broadcasted_iota(jnp.int32, (A_BLOCK, 2 * A_BLOCK), 1)
    steps = A_BLOCK + qi - kj
    band = (steps >= 0) & (steps <= STEPS_BACK)
    dist = (steps * dilation).astype(F32)
    alibi = [jnp.where(band, -slopes_ref[group, h] * dist, -jnp.inf) for h in range(HEADS_PER_GROUP)]
    no_prev = jnp.where((kj >= A_BLOCK) | (n > 0), 0.0, -jnp.inf)
    lane_head = lax.broadcasted_iota(jnp.int32, (1, GROUP_W), 1) // HEAD_DIM_A
    heads = range(HEADS_PER_GROUP)

    def rows_of(ref, rows):
        return jnp.concatenate([ref[0, half, rows, :] for half in range(HALVES)], axis=-1)

    def run(tasks):
        q, k, v, own_rows = [], [], [], []
        for u, r in tasks:
            own = pl.ds(u * span + r, A_BLOCK, stride=dilation)
            if u == 0:
                prev_k, prev_v, prev = kp_ref, vp_ref, pl.ds(r, A_BLOCK, stride=dilation)
            else:
                prev_k, prev_v, prev = ko_ref, vo_ref, pl.ds((u - 1) * span + r, A_BLOCK, stride=dilation)
            own_rows.append(own)
            q.append(rows_of(q_ref, own) * (HEAD_DIM_A ** -0.5))
            k.append(jnp.concatenate([rows_of(prev_k, prev), rows_of(ko_ref, own)], axis=0).astype(BF16))
            v.append(jnp.concatenate([rows_of(prev_v, prev), rows_of(vo_ref, own)], axis=0).astype(BF16))
        chains = [(t, h) for t in range(len(tasks)) for h in heads]
        s = [lax.dot_general(jnp.where(lane_head == h, q[t], 0.0).astype(BF16), k[t], NT_DIMS,
                             preferred_element_type=F32) for t, h in chains]
        s = [s[c] + alibi[h] for c, (t, h) in enumerate(chains)]
        s = [s[c] + no_prev if tasks[t][0] == 0 else s[c] for c, (t, h) in enumerate(chains)]
        m = [x.max(-1, keepdims=True) for x in s]
        p = [jnp.exp(x - mx) for x, mx in zip(s, m)]
        den = [x.sum(-1, keepdims=True) for x in p]
        oh = [jnp.dot((p[c] / den[c]).astype(BF16), v[t], preferred_element_type=F32) for c, (t, h) in enumerate(chains)]
        for t in range(len(tasks)):
            o_acc = jnp.zeros((A_BLOCK, GROUP_W), F32)
            lse_acc = jnp.zeros((A_BLOCK, GROUP_W), F32)
            for h in heads:
                c = t * HEADS_PER_GROUP + h
                o_acc = jnp.where(lane_head == h, oh[c], o_acc)
                lse_acc = jnp.where(lane_head == h, m[c] + jnp.log(den[c]), lse_acc)
            for half in range(HALVES):
                o_ref[0, half, own_rows[t], :] = o_acc[:, half * LANES:(half + 1) * LANES]
                lse_ref[0, half, own_rows[t], :] = lse_acc[:, half * LANES:(half + 1) * LANES]

    if dilation * units <= ATTN_STATIC_BLOCKS:
        blocks = [(u, r) for u in range(units) for r in range(dilation)]
        for j in range(0, len(blocks), tasks):
            run(blocks[j:j + tasks])
    else:
        for u in range(units):
            def residues(j, carry, u=u):
                run([(u, j * tasks + t) for t in range(tasks)])
                return carry
            lax.fori_loop(0, dilation // tasks, residues, 0)


def _attention_prompt(att, slopes, group):
    b, _, s, _ = att.shape
    d = DILATIONS[group]
    span = A_BLOCK * d
    units = max(1, PROMPT_ROW_TILE // span)
    rows = span * units
    tasks = ATTN_TASKS * (2 if d > SUBLANES else 1)
    qb, kb, vb = group, N_GROUPS + group, 2 * N_GROUPS + group
    own = lambda col: pl.BlockSpec((1, HALVES, rows, LANES), lambda i, n: (i, col, n, 0))
    prev = lambda col: pl.BlockSpec((1, HALVES, span, LANES), lambda i, n: (i, col, jnp.maximum(n * units - 1, 0), 0))
    out_spec = pl.BlockSpec((1, HALVES, rows, LANES), lambda i, n: (i, 0, n, 0))
    return pl.pallas_call(
        functools.partial(_attn_prompt_kernel, group=group, dilation=d, units=units, tasks=tasks),
        out_shape=[jax.ShapeDtypeStruct((b, HALVES, s, LANES), F32)] * 2,
        grid=(b, s // rows),
        in_specs=[pl.BlockSpec(memory_space=pltpu.SMEM), own(qb), prev(kb), own(kb), prev(vb), own(vb)],
        out_specs=[out_spec, out_spec],
        compiler_params=_cparams(("arbitrary", "arbitrary")),
        name=f"attn_prompt_g{group}",
    )(slopes, att, att, att, att, att)


def _sample_bias(slopes, group, t_new):
    d = DILATIONS[group]
    lb = WINDOWS[group]
    t = np.arange(t_new)

    def table(key_pos, width):
        diff = (lb + t)[:, None] - key_pos[None, :]
        ok = (diff >= 0) & (diff % d == 0) & (diff // d <= STEPS_BACK)
        dist = jnp.asarray(np.where(ok, diff, 0), F32)
        bias = jnp.where(jnp.asarray(ok)[None], -slopes[group][:, None, None] * dist[None], -jnp.inf)
        bias = bias.reshape(HEADS_PER_GROUP * t_new, key_pos.shape[0])
        return jnp.pad(bias, ((0, 0), (0, width - key_pos.shape[0])), constant_values=-jnp.inf)

    return table(np.arange(lb), lb), table(lb + t, LANES)


def _attn_sample_kernel(q_ref, k_ref, v_ref, cache_ref, bias_ref, biasn_ref, o_ref, lse_ref, *, bb, per_iter, t_new):
    nq = HEADS_PER_GROUP * t_new
    row_head = lax.broadcasted_iota(jnp.int32, (nq, GROUP_W), 0) // t_new
    lane_head2 = lax.broadcasted_iota(jnp.int32, (nq, GROUP_W), 1) // HEAD_DIM_A
    lane_head = lax.broadcasted_iota(jnp.int32, (1, GROUP_W), 1) // HEAD_DIM_A
    pad_n = jnp.zeros((LANES - t_new, GROUP_W), F32)

    def new_rows(ref, i):
        return jnp.concatenate([ref[i, half] for half in range(HALVES)], axis=-1)

    def body(it, carry):
        seqs = [it * per_iter + s for s in range(per_iter)]
        each = lambda f: [f(i) for i in seqs]
        n = range(per_iter)
        q = each(lambda i: new_rows(q_ref, i) * (HEAD_DIM_A ** -0.5))
        qm = [jnp.where(row_head == lane_head2, jnp.concatenate([x] * HEADS_PER_GROUP, axis=0), 0.0).astype(BF16)
              for x in q]
        kt = each(lambda i: cache_ref[0, i, 0].astype(BF16))
        vt = each(lambda i: cache_ref[0, i, 1].astype(BF16))
        kn = each(lambda i: jnp.concatenate([new_rows(k_ref, i), pad_n], axis=0).astype(BF16))
        vn = each(lambda i: jnp.concatenate([new_rows(v_ref, i), pad_n], axis=0).astype(BF16))
        sc = [jnp.dot(qm[c], kt[c], preferred_element_type=F32) + bias_ref[...] for c in n]
        sn = [lax.dot_general(qm[c], kn[c], NT_DIMS, preferred_element_type=F32) + biasn_ref[...] for c in n]
        m = [jnp.maximum(sc[c].max(-1, keepdims=True), sn[c].max(-1, keepdims=True)) for c in n]
        pc = [jnp.exp(sc[c] - m[c]) for c in n]
        pn = [jnp.exp(sn[c] - m[c]) for c in n]
        den = [pc[c].sum(-1, keepdims=True) + pn[c].sum(-1, keepdims=True) for c in n]
        of = [lax.dot_general((pc[c] / den[c]).astype(BF16), vt[c], NT_DIMS, preferred_element_type=F32)
              + jnp.dot((pn[c] / den[c]).astype(BF16), vn[c], preferred_element_type=F32) for c in n]
        for c, i in enumerate(seqs):
            lse_col = m[c] + jnp.log(den[c])
            o = jnp.zeros((t_new, GROUP_W), F32)
            lse = jnp.zeros((t_new, GROUP_W), F32)
            for h in range(HEADS_PER_GROUP):
                head = lane_head == h
                o = jnp.where(head, of[c][h * t_new:(h + 1) * t_new, :], o)
                lse = jnp.where(head, lse_col[h * t_new:(h + 1) * t_new, :], lse)
            for half in range(HALVES):
                o_ref[i, half] = o[:, half * LANES:(half + 1) * LANES]
                lse_ref[i, half] = lse[:, half * LANES:(half + 1) * LANES]
        return carry

    lax.fori_loop(0, bb // per_iter, body, 0)


def _attention_sample(att, cache_t, layer, slopes, group, bb, per_iter):
    nseq, _, t_new, _ = att.shape
    lb = WINDOWS[group]
    bias, bias_new = _sample_bias(slopes, group, t_new)
    qb, kb, vb = group, N_GROUPS + group, 2 * N_GROUPS + group
    new_spec = lambda col: pl.BlockSpec((bb, HALVES, t_new, LANES), lambda i: (i, col, 0, 0))
    out_spec = pl.BlockSpec((bb, HALVES, t_new, LANES), lambda i: (i, 0, 0, 0))
    return pl.pallas_call(
        functools.partial(_attn_sample_kernel, bb=bb, per_iter=per_iter, t_new=t_new),
        out_shape=[jax.ShapeDtypeStruct((nseq, HALVES, t_new, LANES), F32)] * 2,
        grid=(nseq // bb,),
        in_specs=[new_spec(qb), new_spec(kb), new_spec(vb),
                  pl.BlockSpec((1, bb, 2, GROUP_W, lb), lambda i: (layer, i, 0, 0, 0)),
                  pl.BlockSpec(bias.shape, lambda i: (0, 0)), pl.BlockSpec(bias_new.shape, lambda i: (0, 0))],
        out_specs=[out_spec, out_spec],
        compiler_params=_cparams(("arbitrary",)),
        name=f"attn_sample_g{group}",
    )(att, att, att, cache_t, bias, bias_new)


def _unit_lower_inverses(lowers, ri, ci):
    c = lowers[0].shape[0]
    base = min(TRI_BASE, c)
    ident = (ri == ci).astype(F32)
    same = (ri // base) == (ci // base)
    powers = [jnp.where(same, l, 0.0) for l in lowers]
    invs = [ident - p for p in powers]
    for _ in range(int(np.log2(base)) - 1):
        powers = [_bdot(p, p) for p in powers]
        invs = [t + _bdot(t, p) for t, p in zip(invs, powers)]
    size = base
    while size < c:
        pair = ((ri // (2 * size)) == (ci // (2 * size))) & ((ri // size) % 2 == 1) & ((ci // size) % 2 == 0)
        offs = [_bdot(jnp.where(pair, l, 0.0), t) for l, t in zip(lowers, invs)]
        invs = [t - _bdot(t, o) for t, o in zip(invs, offs)]
        size *= 2
    return invs


def _gdn_kernel(*refs, bb, per_iter, chunk, layered, single_chunk, aliased):
    (q_ref, k_ref, v_ref, ab_ref, conv0_ref, s0_ref, cw_ref, par_ref), refs = refs[:8], refs[8 + int(aliased):]
    o_ref, s_ref, ext_ref = refs
    c_idx = pl.program_id(1)
    C = chunk
    first_row = SUBLANES - (CONV_WIDTH - 1)
    if layered:
        s0_ref, s_ref = s0_ref.at[0], s_ref.at[0]
    cw_ref, par_ref = cw_ref.at[0], par_ref.at[0]
    s_prev = s0_ref if single_chunk else s_ref

    @pl.when(c_idx == 0)
    def _():
        ext_ref[:, 0:SUBLANES, :] = conv0_ref[...]
        if not single_chunk:
            s_ref[...] = s0_ref[...]

    @pl.when(c_idx > 0)
    def _():
        ext_ref[:, 0:SUBLANES, :] = ext_ref[:, C:C + SUBLANES, :]

    ext_ref[:, SUBLANES:SUBLANES + C, 0:WIDTH_B] = q_ref[...].astype(F32)
    ext_ref[:, SUBLANES:SUBLANES + C, WIDTH_B:2 * WIDTH_B] = k_ref[...].astype(F32)
    ext_ref[:, SUBLANES:SUBLANES + C, 2 * WIDTH_B:3 * WIDTH_B] = v_ref[...].astype(F32)

    ri = lax.broadcasted_iota(jnp.int32, (C, C), 0)
    ci = lax.broadcasted_iota(jnp.int32, (C, C), 1)
    causal = ri >= ci
    strict = ri > ci
    tril = causal.astype(F32)
    neg_a = -jnp.exp(par_ref[0:1, :])
    dt_bias = par_ref[1:2, :]
    norm_w = par_ref[2:3, :]
    pad_rows = jnp.zeros((HEAD_DIM_B - C, HEAD_DIM_B), F32) if C < HEAD_DIM_B else None

    def pad128(x):
        return x if pad_rows is None else jnp.concatenate([x, pad_rows], axis=0)

    mxu_shift = q_ref.dtype == BF16 and C > SUBLANES
    if mxu_shift:
        sr = lax.broadcasted_iota(jnp.int32, ((CONV_WIDTH - 1) * C, C), 0)
        sc = lax.broadcasted_iota(jnp.int32, ((CONV_WIDTH - 1) * C, C), 1)
        shift_mat = (sc == (sr % C) - (CONV_WIDTH - 1 - sr // C)).astype(BF16)

    def shifted(i):
        if not mxu_shift:
            return None
        return [jnp.dot(shift_mat, ref[i], preferred_element_type=F32) for ref in (q_ref, k_ref, v_ref)]

    def taps(i, rows, n_rows, col):
        acc = ext_ref[i, pl.ds(rows, n_rows), pl.ds(col, HEAD_DIM_B)] * cw_ref[0:1, pl.ds(col, HEAD_DIM_B)]
        for j in range(1, CONV_WIDTH):
            acc = acc + (ext_ref[i, pl.ds(rows + j, n_rows), pl.ds(col, HEAD_DIM_B)]
                         * cw_ref[j:j + 1, pl.ds(col, HEAD_DIM_B)])
        return acc

    def conv(i, col, sh=None):
        if sh is None:
            return _silu_of_half(taps(i, first_row, C, col))
        part, within = sh[col // WIDTH_B], col % WIDTH_B
        acc = ext_ref[i, pl.ds(SUBLANES, C), pl.ds(col, HEAD_DIM_B)] * cw_ref[CONV_WIDTH - 1:CONV_WIDTH,
                                                                              pl.ds(col, HEAD_DIM_B)]
        for j in range(CONV_WIDTH - 1):
            acc = acc + part[j * C:(j + 1) * C, within:within + HEAD_DIM_B] * cw_ref[j:j + 1, pl.ds(col, HEAD_DIM_B)]
        acc = jnp.concatenate([taps(i, first_row, SUBLANES, col), acc[SUBLANES:]], axis=0)
        return _silu_of_half(acc)

    def l2n(x, scale=1.0):
        return x * (lax.rsqrt(jnp.sum(x * x, axis=-1, keepdims=True) + RMS_EPS) * scale)

    def gates(i):
        ab = ab_ref[i, 0]
        z = ab + dt_bias
        softplus = jnp.maximum(z, 0.0) + jnp.log(1.0 + jnp.exp(-jnp.abs(z)))
        g_all = neg_a * softplus
        beta_all = _sigmoid(ab)
        gcum_all = jnp.dot(tril, g_all, preferred_element_type=F32, precision=HIGHEST)
        return beta_all, gcum_all, pad128(gcum_all).T

    def body(it, carry):
        seqs = [it * per_iter + s for s in range(per_iter)]
        beta_all, gcum_all, gcum_rows = zip(*[gates(i) for i in seqs])
        chains = [(s, h) for s in range(per_iter) for h in range(N_HEADS_B)]
        each = lambda f: [f(s, h) for s, h in chains]
        n = range(len(chains))
        states = each(lambda s, h: s_prev[seqs[s], h])
        sh = [shifted(i) for i in seqs]
        q = each(lambda s, h: l2n(conv(seqs[s], h * HEAD_DIM_B, sh[s]), HEAD_DIM_B ** -0.5))
        k = each(lambda s, h: l2n(conv(seqs[s], WIDTH_B + h * HEAD_DIM_B, sh[s])))
        v = each(lambda s, h: conv(seqs[s], 2 * WIDTH_B + h * HEAD_DIM_B, sh[s]))
        beta = each(lambda s, h: beta_all[s][:, N_HEADS_B + h:N_HEADS_B + h + 1])
        gcum = each(lambda s, h: gcum_all[s][:, h:h + 1])
        grow = each(lambda s, h: gcum_rows[s][h:h + 1, 0:C])
        decay = [jnp.exp(jnp.where(causal, gcum[c] - grow[c], -jnp.inf)) for c in n]
        qk = [_bdot_nt(jnp.concatenate([q[c], k[c]], axis=0), k[c]) for c in n]
        intra = [qk[c][0:C] * decay[c] for c in n]
        lower = [jnp.where(strict, qk[c][C:2 * C] * beta[c] * decay[c], 0.0) for c in n]
        inv = _unit_lower_inverses(lower, ri, ci)
        e_g = [jnp.exp(gcum[c]) for c in n]
        uw = [_bdot(inv[c], jnp.concatenate([v[c] * beta[c], k[c] * (beta[c] * e_g[c])], axis=-1)) for c in n]
        ws_qs = [_bdot(jnp.concatenate([uw[c][:, HEAD_DIM_B:2 * HEAD_DIM_B], q[c] * e_g[c]], axis=0), states[c])
                 for c in n]
        v_new = [uw[c][:, 0:HEAD_DIM_B] - ws_qs[c][0:C] for c in n]
        g_last = [gcum[c][C - 1:C, :] for c in n]
        k_dec = [k[c] * jnp.exp(g_last[c] - gcum[c]) for c in n]
        o = [ws_qs[c][C:2 * C] + _bdot(intra[c], v_new[c]) for c in n]
        new_states = [states[c] * jnp.exp(g_last[c]) + _bdot(pad128(k_dec[c]).T, pad128(v_new[c])) for c in n]
        for c, (s, h) in enumerate(chains):
            s_ref[seqs[s], h] = new_states[c]
            o_c = o[c] * lax.rsqrt(jnp.mean(o[c] * o[c], axis=-1, keepdims=True) + RMS_EPS) * norm_w
            o_ref[seqs[s], :, pl.ds(h * HEAD_DIM_B, HEAD_DIM_B)] = o_c
        return carry

    lax.fori_loop(0, bb // per_iter, body, 0)


def _gated_deltanet(rest, att, conv0, s0, layer, conv_w, params, bb, per_iter, chunk, state_buf=None):
    nseq, t, _ = rest.shape
    qb = REST_QKVB // WIDTH_B
    col_spec = lambda col: pl.BlockSpec((bb, chunk, WIDTH_B), lambda i, c: (i, c, col))
    state_shape = (bb, N_HEADS_B, HEAD_DIM_B, HEAD_DIM_B)
    layered = s0.ndim == 5
    if layered:
        state_spec = pl.BlockSpec((1,) + state_shape, lambda i, c: (layer, i, 0, 0, 0))
        state_out = jax.ShapeDtypeStruct(s0.shape, F32)
    else:
        state_spec = pl.BlockSpec(state_shape, lambda i, c: (i, 0, 0, 0))
        state_out = jax.ShapeDtypeStruct((nseq,) + state_shape[1:], F32)
    in_specs = [col_spec(qb), col_spec(qb + 1), col_spec(qb + 2),
                pl.BlockSpec((bb, 1, chunk, LANES), lambda i, c: (i, AB_BLOCK, c, 0)),
                pl.BlockSpec((bb, SUBLANES, CONV_DIM), lambda i, c: (i, 0, 0)),
                state_spec,
                pl.BlockSpec((1, SUBLANES, CONV_DIM), lambda i, c: (layer, 0, 0)),
                pl.BlockSpec((1, SUBLANES, LANES), lambda i, c: (layer, 0, 0))]
    args = [rest, rest, rest, att, conv0, s0, conv_w, params]
    aliases = {}
    if state_buf is not None:
        aliases = {len(args): 1}
        in_specs.append(pl.BlockSpec(memory_space=pl.ANY))
        args.append(state_buf)
    return pl.pallas_call(
        functools.partial(_gdn_kernel, bb=bb, per_iter=per_iter, chunk=chunk, layered=layered,
                          single_chunk=(t == chunk), aliased=state_buf is not None),
        out_shape=[jax.ShapeDtypeStruct((nseq, t, WIDTH_B), F32), state_out],
        grid=(nseq // bb, t // chunk),
        in_specs=in_specs,
        out_specs=[pl.BlockSpec((bb, chunk, WIDTH_B), lambda i, c: (i, c, 0)), state_spec],
        scratch_shapes=[pltpu.VMEM((bb, chunk + SUBLANES, CONV_DIM), F32)],
        input_output_aliases=aliases,
        compiler_params=_cparams(("arbitrary", "arbitrary")),
        name="gated_deltanet",
    )(*args)


def _out_kernel(x_ref, mod_ref, o1_ref, o2_ref, o3_ref, l1_ref, l2_ref, l3_ref, za_ref, ob_ref, zb_ref, ga_ref, gb_ref,
                wa_ref, wb_ref, wo_ref, lng_ref, lnb_ref, y_ref):
    bs, bt, _ = x_ref.shape
    rows = bs * bt
    flat = lambda ref: ref[...].reshape(rows, ref.shape[-1]).astype(F32)
    group = lambda ref: jnp.concatenate([ref[:, half].reshape(rows, LANES) for half in range(HALVES)], axis=-1)
    l1, l2, l3 = group(l1_ref), group(l2_ref), group(l3_ref)
    m = jnp.maximum(jnp.maximum(l1, l2), l3)
    e1, e2, e3 = jnp.exp(l1 - m), jnp.exp(l2 - m), jnp.exp(l3 - m)
    inv_tot = 1.0 / (e1 + e2 + e3)
    att = jnp.concatenate([group(o1_ref) * (e1 * inv_tot), group(o2_ref) * (e2 * inv_tot),
                           group(o3_ref) * (e3 * inv_tot)], axis=-1)
    half_a = jnp.dot((att * _silu_of_half(flat(za_ref))).astype(BF16), wa_ref[0], preferred_element_type=F32)
    half_b = jnp.dot((flat(ob_ref) * _silu_of_half(flat(zb_ref))).astype(BF16), wb_ref[0], preferred_element_type=F32)
    merged = (half_a + half_a * jnp.tanh(flat(ga_ref))) + (half_b + half_b * jnp.tanh(flat(gb_ref)))
    out = jnp.dot(merged.astype(BF16), wo_ref[0], preferred_element_type=F32).reshape(bs, bt, D_MODEL)
    gate = mod_ref[:, :, 2 * D_MODEL:3 * D_MODEL]
    y = DEEPNORM_ALPHA * x_ref[...] + gate * out
    mu = jnp.mean(y, axis=-1, keepdims=True)
    yc = y - mu
    var = jnp.mean(yc * yc, axis=-1, keepdims=True)
    y_ref[...] = yc * lax.rsqrt(var + LN_EPS) * lng_ref[...] + lnb_ref[...]


def _output_block(x, mod, o_g, lse_g, o_b, rest, wa, wb, wo, ln_g, ln_b, layer, bs, bt):
    nseq, t, _ = x.shape
    row = lambda w, col: pl.BlockSpec((bs, bt, w), lambda i, k: (i, k, col))
    halves = pl.BlockSpec((bs, HALVES, bt, LANES), lambda i, k: (i, 0, k, 0))
    per_layer = lambda shape: pl.BlockSpec((1,) + shape, lambda i, k: (layer,) + (0,) * len(shape))
    return pl.pallas_call(
        _out_kernel,
        out_shape=jax.ShapeDtypeStruct((nseq, t, D_MODEL), F32),
        grid=(nseq // bs, t // bt),
        in_specs=[row(D_MODEL, 0),
                  pl.BlockSpec((bs, 1, 3 * D_MODEL), lambda i, k: (i, 0, 0)),
                  halves, halves, halves, halves, halves, halves,
                  row(WIDTH_A, REST_ZA // WIDTH_A),
                  row(WIDTH_B, 0),
                  row(WIDTH_B, REST_ZB // WIDTH_B),
                  row(D_MODEL, REST_GA // D_MODEL),
                  row(D_MODEL, REST_GB // D_MODEL),
                  per_layer((WIDTH_A, D_MODEL)), per_layer((WIDTH_B, D_MODEL)), per_layer((D_MODEL, D_MODEL)),
                  per_layer((1, D_MODEL)), per_layer((1, D_MODEL))],
        out_specs=row(D_MODEL, 0),
        compiler_params=_cparams(("arbitrary", "arbitrary")),
        name="output_block",
    )(x, mod, o_g[0], o_g[1], o_g[2], lse_g[0], lse_g[1], lse_g[2], rest, o_b, rest, rest, rest,
      wa, wb, wo, ln_g, ln_b)


def _alibi_slopes():
    i = jnp.arange(1, N_HEADS_A + 1, dtype=F32)
    return jnp.exp2(-8.0 * i / N_HEADS_A).reshape(N_GROUPS, HEADS_PER_GROUP)


def _split_w_in(w_in):
    zeros = lambda n: jnp.zeros(w_in.shape[:2] + (n,), BF16)
    att = jnp.concatenate([w_in[:, :, :SRC_ZA].astype(BF16), w_in[:, :, SRC_AB:SRC_GA].astype(BF16),
                           zeros(LANES - 2 * N_HEADS_B)], axis=-1)
    rest = jnp.concatenate([w_in[:, :, SRC_ZA:SRC_AB].astype(BF16), zeros(REST_GA - (SRC_AB - SRC_ZA)),
                            w_in[:, :, SRC_GA:].astype(BF16)], axis=-1)
    return att, rest


def _split_kv(att, group, start):
    b = att.shape[0]
    k = att[:, (N_GROUPS + group) * HALVES:(N_GROUPS + group + 1) * HALVES, start:, :]
    v = att[:, (2 * N_GROUPS + group) * HALVES:(2 * N_GROUPS + group + 1) * HALVES, start:, :]
    kv = jnp.stack([k, v], axis=1)
    kv = kv.transpose(0, 3, 1, 2, 4)
    return kv.reshape(b, kv.shape[1], 2, HEADS_PER_GROUP, HEAD_DIM_A)


def _feature_major(cache):
    d, b, l = cache.shape[:3]
    return cache.transpose(0, 1, 3, 4, 5, 2).reshape(d, b, 2, GROUP_W, l)


def kernel(x_prompt, x_sample, c_prompt, c_sample, cache_win1, cache_win2, cache_win3, state_gdn, state_conv,
           w_ada, b_ada, w_in, conv_w, a_log, dt_bias, gdn_norm, w_out_a, w_out_b, w_o, ln_g, ln_b):
    n_p, seq, _ = x_prompt.shape
    n_s, t_new, _ = x_sample.shape
    caches = tuple(_feature_major(c) for c in (cache_win1, cache_win2, cache_win3))
    sample_attn_tiling = ((16, 4), (16, 2), (4, 2))
    slopes = _alibi_slopes()

    w_att, w_rest = _split_w_in(w_in)
    wa = (0.5 * w_out_a).astype(BF16)
    wb = (0.5 * w_out_b).astype(BF16)
    wo = w_o.astype(BF16)
    ln_g3 = ln_g.reshape(DEPTH, 1, D_MODEL)
    ln_b3 = ln_b.reshape(DEPTH, 1, D_MODEL)
    conv_w_p = jnp.pad(0.5 * conv_w, ((0, 0), (0, SUBLANES - CONV_WIDTH), (0, 0)))
    params = jnp.zeros((DEPTH, SUBLANES, LANES), F32)
    params = params.at[:, 0, :N_HEADS_B].set(a_log).at[:, 1, :N_HEADS_B].set(dt_bias).at[:, 2, :].set(gdn_norm)

    n_all = n_p + n_s
    n_pad = -(-n_all // SUBLANES) * SUBLANES
    c_all = jnp.pad(jnp.concatenate([c_prompt, c_sample], axis=0), ((0, n_pad - n_all), (0, 0)))
    mod_all = _modulation(c_all, w_ada, b_ada)

    zero_conv = jnp.zeros((n_p, SUBLANES, CONV_DIM), F32)
    zero_state = jnp.zeros((n_p, N_HEADS_B, HEAD_DIM_B, HEAD_DIM_B), F32)
    conv_s = jnp.pad(state_conv, ((0, 0), (0, 0), (SUBLANES - (CONV_WIDTH - 1), 0), (0, 0)))

    xp, xs = x_prompt, x_sample
    acc_p = [[] for _ in range(N_GROUPS + 2)]
    acc_s = [[] for _ in range(N_GROUPS + 2)]
    gdn_sample = None
    for l in range(DEPTH):
        mod_p = mod_all[l, :n_p, None, :]
        mod_s = mod_all[l, n_p:n_all, None, :]

        att_p, rest_p = _in_projection(xp, mod_p, w_att, w_rest, l, 1, PROMPT_PROJ_ROW_TILE, BF16)
        att_s, rest_s = _in_projection(xs, mod_s, w_att, w_rest, l, SAMPLE_SEQ_TILE, t_new, F32)

        o_p, lse_p, o_s, lse_s = [], [], [], []
        for g in range(N_GROUPS):
            o, lse = _attention_prompt(att_p, slopes, g)
            o_p.append(o)
            lse_p.append(lse)
            o, lse = _attention_sample(att_s, caches[g], l, slopes, g, *sample_attn_tiling[g])
            o_s.append(o)
            lse_s.append(lse)

        ob_p, st_p = _gated_deltanet(rest_p, att_p, zero_conv, zero_state, l, conv_w_p, params, n_p,
                                     PROMPT_GDN_SEQS, PROMPT_GDN_CHUNK)
        ob_s, gdn_sample = _gated_deltanet(rest_s, att_s, conv_s[l], state_gdn, l, conv_w_p, params,
                                           *SAMPLE_GDN_TILING, t_new, state_buf=gdn_sample)

        xp_new = _output_block(xp, mod_p, o_p, lse_p, ob_p, rest_p, wa, wb, wo, ln_g3, ln_b3, l,
                               1, PROMPT_OUT_ROW_TILE)
        xs_new = _output_block(xs, mod_s, o_s, lse_s, ob_s, rest_s, wa, wb, wo, ln_g3, ln_b3, l,
                               SAMPLE_SEQ_TILE, t_new)

        for g in range(N_GROUPS):
            acc_p[g].append(_split_kv(att_p, g, seq - min(WINDOWS[g], seq)))
            acc_s[g].append(_split_kv(att_s, g, 0))
        acc_p[N_GROUPS].append(st_p)
        acc_p[N_GROUPS + 1].append(rest_p[:, seq - (CONV_WIDTH - 1):, REST_QKVB:REST_QKVB + CONV_DIM].astype(F32))
        acc_s[N_GROUPS + 1].append(rest_s[:, t_new - (CONV_WIDTH - 1):, REST_QKVB:REST_QKVB + CONV_DIM])
        xp, xs = xp_new, xs_new

    acc_s[N_GROUPS] = None
    outs_p = [jnp.stack(a) for a in acc_p]
    outs_s = [gdn_sample if a is None else jnp.stack(a) for a in acc_s]
    return (xp, xs, *outs_p, *outs_s)
```
